```python
import math
import jax, jax.numpy as jnp
from jax import lax
import numpy as np

D_MODEL = 1024
BATCH = 2
SEQ = 8192
DEPTH = 2
DEC_BATCH = 128
DEC_SEQ = 4
PAST_LEN = 2048
PAGE_SIZE = 128

N_MIXERS = 2
N_CONV_LAYERS = (DEPTH + 1) // 2
N_ATTN_LAYERS = DEPTH // 2
N_HEADS = 16
HEAD_DIM = D_MODEL // N_HEADS
CONV_WIDTH = 3
D_FF = 2816
Q_BLOCK = 128
NORM_EPS = 1e-6
SB_BIAS_MEAN = -6.0

kernel_name = "hybrid_shortconv_stickbreaking_convffn_step"


def rmsnorm(x, g):
    xf = x.astype(jnp.float32)
    y = xf * lax.rsqrt(jnp.mean(xf * xf, axis=-1, keepdims=True) + NORM_EPS) * g.astype(jnp.float32)
    return y.astype(x.dtype)


def causal_dwconv(u, prev, w):
    T = u.shape[1]
    ext = jnp.concatenate([prev.astype(u.dtype), u], axis=1)
    y = ext[:, 0:T] * w[0]
    for j in range(1, CONV_WIDTH):
        y = y + ext[:, j:j + T] * w[j]
    return y, ext[:, -(CONV_WIDTH - 1):]


def short_conv_mixer(h, prev, w_in, w_dw, w_out):
    proj = h @ w_in
    b_gate, c_gate, v = jnp.split(proj, 3, axis=-1)
    u = c_gate * v
    uc, st = causal_dwconv(u, prev, w_dw)
    return (b_gate * uc) @ w_out, st


def stick_breaking(q, k, v, q_pos, k_pos, bias):
    scale = 1.0 / math.sqrt(HEAD_DIM)
    z = jnp.einsum('bqhd,bkhd->bhqk', q.astype(jnp.float32), k.astype(jnp.float32)) * scale
    z = z + bias.astype(jnp.float32)[None, :, None, None]
    mask = (k_pos[None, :] < q_pos[:, None])[None, None]
    log_keep = jnp.where(mask, jax.nn.log_sigmoid(-z), 0.0)
    after = lax.cumsum(log_keep, axis=3, reverse=True) - log_keep
    w = jnp.where(mask, jnp.exp(jax.nn.log_sigmoid(z) + after), 0.0)
    o = jnp.einsum('bhqk,bkhd->bqhd', w, v.astype(jnp.float32))
    return o.astype(q.dtype)


def sb_mixer(h, w_qkv, w_o, sb_bias, k_past, v_past, q_pos, k_pos):
    B, T, _ = h.shape
    qkv = (h @ w_qkv).reshape(B, T, 3, N_HEADS, HEAD_DIM)
    q, k, v = qkv[:, :, 0], qkv[:, :, 1], qkv[:, :, 2]
    if k_past is None:
        k_all, v_all = k, v
    else:
        k_all = jnp.concatenate([k_past.astype(k.dtype), k], axis=1)
        v_all = jnp.concatenate([v_past.astype(v.dtype), v], axis=1)
    if T % Q_BLOCK == 0:
        n_blk = T // Q_BLOCK
        qb = jnp.moveaxis(q.reshape(B, n_blk, Q_BLOCK, N_HEADS, HEAD_DIM), 1, 0)
        pb = q_pos.reshape(n_blk, Q_BLOCK)
        ob = lax.map(lambda a: stick_breaking(a[0], k_all, v_all, a[1], k_pos, sb_bias), (qb, pb))
        o = jnp.moveaxis(ob, 0, 1).reshape(B, T, D_MODEL)
    else:
        o = stick_breaking(q, k_all, v_all, q_pos, k_pos, sb_bias).reshape(B, T, D_MODEL)
    return o @ w_o, k, v


def conv_ffn(h, prev, w_up, w_dw, b_dw, w_down):
    up = h @ w_up
    g, u = jnp.split(up, 2, axis=-1)
    gc, st = causal_dwconv(g, prev, w_dw)
    a = jax.nn.silu(gc + b_dw) * u
    return a @ w_down, st


def trunk(x, q_pos, k_pos, conv_prev, ffn_prev, k_past, v_past,
          norm_mix, norm_ffn, norm_final, w_sc_in, w_sc_dw, w_sc_out,
          w_qkv, w_o, sb_bias, w_ffn_up, w_ffn_dw, b_ffn_dw, w_ffn_down):
    conv_new, ffn_new, k_new, v_new = [], [], [], []
    for i in range(DEPTH):
        a = i // N_MIXERS
        h = rmsnorm(x, norm_mix[i])
        if i % N_MIXERS == 0:
            y, st = short_conv_mixer(h, conv_prev[a], w_sc_in[a], w_sc_dw[a], w_sc_out[a])
            conv_new.append(st)
        else:
            kp = None if k_past is None else k_past[a]
            vp = None if v_past is None else v_past[a]
            y, kn, vn = sb_mixer(h, w_qkv[a], w_o[a], sb_bias[a], kp, vp, q_pos, k_pos)
            k_new.append(kn)
            v_new.append(vn)
        x = x + y
        h = rmsnorm(x, norm_ffn[i])
        y, st = conv_ffn(h, ffn_prev[i], w_ffn_up[i], w_ffn_dw[i], b_ffn_dw[i], w_ffn_down[i])
        ffn_new.append(st)
        x = x + y
    return (rmsnorm(x, norm_final), jnp.stack(conv_new), jnp.stack(ffn_new),
            jnp.stack(k_new), jnp.stack(v_new))


def setup_inputs(seed: int = 0) -> dict:
    key = jax.random.key(seed)
    ks = jax.random.split(key, 24)
    n_pages = PAST_LEN // PAGE_SIZE
    n_used = DEC_BATCH * n_pages
    n_pool = (5 * n_used + 3) // 4
    nrm = jax.random.normal
    f32 = jnp.float32
    d, w = D_MODEL, CONV_WIDTH
    perm = jax.random.permutation(ks[0], n_pool)[:n_used]
    page_table = perm.reshape(DEC_BATCH, n_pages).astype(jnp.int32)
    return {
        "x_prompt": nrm(ks[1], (BATCH, SEQ, d), f32),
        "x_sample": nrm(ks[2], (DEC_BATCH, DEC_SEQ, d), f32),
        "cache_k": nrm(ks[3], (N_ATTN_LAYERS, n_pool, PAGE_SIZE, N_HEADS, HEAD_DIM), f32),
        "cache_v": nrm(ks[4], (N_ATTN_LAYERS, n_pool, PAGE_SIZE, N_HEADS, HEAD_DIM), f32),
        "state_conv": nrm(ks[5], (N_CONV_LAYERS, DEC_BATCH, w - 1, d), f32),
        "state_ffn": nrm(ks[6], (DEPTH, DEC_BATCH, w - 1, D_FF), f32),
        "page_table": page_table,
        "norm_mix": 1.0 + 0.02 * nrm(ks[7], (DEPTH, d), f32),
        "norm_ffn": 1.0 + 0.02 * nrm(ks[8], (DEPTH, d), f32),
        "norm_final": 1.0 + 0.02 * nrm(ks[9], (d,), f32),
        "w_sc_in": nrm(ks[10], (N_CONV_LAYERS, d, 3 * d), f32) * d ** -0.5,
        "w_sc_dw": nrm(ks[11], (N_CONV_LAYERS, w, d), f32) * w ** -0.5,
        "w_sc_out": nrm(ks[12], (N_CONV_LAYERS, d, d), f32) * d ** -0.5,
        "w_qkv": nrm(ks[13], (N_ATTN_LAYERS, d, 3 * d), f32) * d ** -0.5,
        "w_o": nrm(ks[14], (N_ATTN_LAYERS, d, d), f32) * d ** -0.5,
        "sb_bias": SB_BIAS_MEAN + 0.5 * nrm(ks[19], (N_ATTN_LAYERS, N_HEADS), f32),
        "w_ffn_up": nrm(ks[15], (DEPTH, d, 2 * D_FF), f32) * d ** -0.5,
        "w_ffn_dw": nrm(ks[16], (DEPTH, w, D_FF), f32) * w ** -0.5,
        "b_ffn_dw": 0.01 * nrm(ks[17], (DEPTH, D_FF), f32),
        "w_ffn_down": nrm(ks[18], (DEPTH, D_FF, d), f32) * D_FF ** -0.5,
    }


def reference(x_prompt, x_sample, cache_k, cache_v, state_conv, state_ffn, page_table,
              norm_mix, norm_ffn, norm_final, w_sc_in, w_sc_dw, w_sc_out,
              w_qkv, w_o, sb_bias, w_ffn_up, w_ffn_dw, b_ffn_dw, w_ffn_down):
    bp, tp, _ = x_prompt.shape
    pos_p = jnp.arange(tp, dtype=jnp.int32)
    conv0 = jnp.zeros((N_CONV_LAYERS, bp, CONV_WIDTH - 1, D_MODEL), x_prompt.dtype)
    ffn0 = jnp.zeros((DEPTH, bp, CONV_WIDTH - 1, D_FF), x_prompt.dtype)
    y_prompt, new_conv_prompt, new_ffn_prompt, new_k_prompt, new_v_prompt = trunk(
        x_prompt, pos_p, pos_p, conv0, ffn0, None, None,
        norm_mix, norm_ffn, norm_final, w_sc_in, w_sc_dw, w_sc_out,
        w_qkv, w_o, sb_bias, w_ffn_up, w_ffn_dw, b_ffn_dw, w_ffn_down)

    bs, ts, _ = x_sample.shape
    past = page_table.shape[1] * cache_k.shape[2]
    k_past = cache_k[:, page_table].reshape(cache_k.shape[0], bs, past, N_HEADS, HEAD_DIM)
    v_past = cache_v[:, page_table].reshape(cache_v.shape[0], bs, past, N_HEADS, HEAD_DIM)
    q_pos_s = past + jnp.arange(ts, dtype=jnp.int32)
    k_pos_s = jnp.arange(past + ts, dtype=jnp.int32)
    y_sample, new_conv_sample, new_ffn_sample, new_k_sample, new_v_sample = trunk(
        x_sample, q_pos_s, k_pos_s, state_conv, state_ffn, k_past, v_past,
        norm_mix, norm_ffn, norm_final, w_sc_in, w_sc_dw, w_sc_out,
        w_qkv, w_o, sb_bias, w_ffn_up, w_ffn_dw, b_ffn_dw, w_ffn_down)

    return (y_prompt, y_sample, new_k_prompt, new_v_prompt, new_conv_prompt, new_ffn_prompt,
            new_k_sample, new_v_sample, new_conv_sample, new_ffn_sample)
```

```python
import functools
import math

import jax
import jax.numpy as jnp
from jax import lax
from jax.experimental import pallas as pl
from jax.experimental.pallas import tpu as pltpu

F32 = jnp.float32
BF16 = jnp.bfloat16

NORM_EPS = 1e-6
CONV_WIDTH = 3
N_HEADS = 16
SUBLANES = 8
LANES = 128
VMEM_LIMIT = 56 * 1024 * 1024

_NT = (((1,), (1,)), ((), ()))


def _rmsnorm(x, g):
    ms = jnp.mean(x * x, axis=-1, keepdims=True)
    return x * lax.rsqrt(ms + NORM_EPS) * g


def _dot(a, b):
    return jnp.dot(a, b, preferred_element_type=F32)


def _const_spec(shape):
    return pl.BlockSpec(shape, lambda *_: (0,) * len(shape), pipeline_mode=pl.Buffered(1))


def _conv_rows(ext_ref, cur, w_ref, *, stride, pad, tm):
    ext_ref[pad:pad + tm, :] = cur
    lo = pad - 2 * stride
    y = ext_ref[lo:lo + tm, :] * w_ref[0:1, :]
    y = y + ext_ref[lo + stride:lo + stride + tm, :] * w_ref[1:2, :]
    return y + cur * w_ref[2:3, :]


def _roll_history(ext_ref, st_ref, *, stride, pad, tm):
    hist = ext_ref[pad + tm - 2 * stride:pad + tm, :]
    st_ref[0] = hist
    ext_ref[pad - 2 * stride:pad, :] = hist


def _mixer0_kernel(x_ref, prev_ref, g_ref, win_ref, wdw_ref, wout_ref,
                   y_ref, st_ref, ext_ref, *, stride, pad, tm, d):
    @pl.when(pl.program_id(1) == 0)
    def _():
        ext_ref[pad - 2 * stride:pad, :] = prev_ref[0]

    x = x_ref[0]
    h = _rmsnorm(x, g_ref[...]).astype(BF16)
    b_gate = _dot(h, win_ref[:, 0:d])
    u = _dot(h, win_ref[:, d:2 * d]) * _dot(h, win_ref[:, 2 * d:3 * d])
    uc = _conv_rows(ext_ref, u, wdw_ref, stride=stride, pad=pad, tm=tm)
    y_ref[0] = x + _dot((b_gate * uc).astype(BF16), wout_ref[...])
    _roll_history(ext_ref, st_ref, stride=stride, pad=pad, tm=tm)


def _mixer0(x, prev, g, w_in, w_dw, w_out, *, stride, tm):
    groups, rows, d = x.shape
    pad = -(-2 * stride // SUBLANES) * SUBLANES
    kern = functools.partial(_mixer0_kernel, stride=stride, pad=pad, tm=tm, d=d)
    return pl.pallas_call(
        kern,
        grid=(groups, rows // tm),
        in_specs=[
            pl.BlockSpec((1, tm, d), lambda gi, t: (gi, t, 0)),
            pl.BlockSpec((1, 2 * stride, d), lambda gi, t: (gi, 0, 0)),
            _const_spec((1, d)),
            _const_spec((d, 3 * d)),
            _const_spec((CONV_WIDTH, d)),
            _const_spec((d, d)),
        ],
        out_specs=[
            pl.BlockSpec((1, tm, d), lambda gi, t: (gi, t, 0)),
            pl.BlockSpec((1, 2 * stride, d), lambda gi, t: (gi, 0, 0)),
        ],
        out_shape=[
            jax.ShapeDtypeStruct((groups, rows, d), F32),
            jax.ShapeDtypeStruct((groups, 2 * stride, d), F32),
        ],
        scratch_shapes=[pltpu.VMEM((pad + tm, d), F32)],
        compiler_params=pltpu.CompilerParams(
            dimension_semantics=("arbitrary", "arbitrary"), vmem_limit_bytes=VMEM_LIMIT),
        name="mixer0",
    )(x, prev, g, w_in, w_dw, w_out)


def _ffn_kernel(*refs, stride, pad, tm, dff, oproj, final):
    refs = list(refs)
    x_ref = refs.pop(0)
    if oproj:
        o_ref, wo_ref = refs.pop(0), refs.pop(0)
    prev_ref, g_ref, wup_ref, wdw_ref, bdw_ref, wdown_ref = refs[:6]
    refs = refs[6:]
    if final:
        gf_ref = refs.pop(0)
    y_ref, st_ref, ext_ref = refs

    @pl.when(pl.program_id(1) == 0)
    def _():
        ext_ref[pad - 2 * stride:pad, :] = prev_ref[0]

    x = x_ref[0]
    if oproj:
        x = x + _dot(o_ref[0], wo_ref[...])
    h = _rmsnorm(x, g_ref[...]).astype(BF16)
    gate = _dot(h, wup_ref[:, 0:dff])
    up = _dot(h, wup_ref[:, dff:2 * dff])
    gc = _conv_rows(ext_ref, gate, wdw_ref, stride=stride, pad=pad, tm=tm) + bdw_ref[...]
    act = gc / (1.0 + jnp.exp(-gc)) * up
    xo = x + _dot(act.astype(BF16), wdown_ref[...])
    y_ref[0] = _rmsnorm(xo, gf_ref[...]) if final else xo
    _roll_history(ext_ref, st_ref, stride=stride, pad=pad, tm=tm)


def _ffn(x, prev, g, w_up, w_dw, b_dw, w_down, *, stride, tm, oproj=None, g_final=None):
    groups, rows, d = x.shape
    dff = w_down.shape[0]
    pad = -(-2 * stride // SUBLANES) * SUBLANES
    kern = functools.partial(_ffn_kernel, stride=stride, pad=pad, tm=tm, dff=dff,
                             oproj=oproj is not None, final=g_final is not None)
    row_spec = pl.BlockSpec((1, tm, d), lambda gi, t: (gi, t, 0))
    args, in_specs = [x], [row_spec]
    if oproj is not None:
        args += list(oproj)
        in_specs += [row_spec, _const_spec((d, d))]
    args += [prev, g, w_up, w_dw, b_dw, w_down]
    in_specs += [
        pl.BlockSpec((1, 2 * stride, dff), lambda gi, t: (gi, 0, 0)),
        _const_spec((1, d)),
        _const_spec((d, 2 * dff)),
        _const_spec((CONV_WIDTH, dff)),
        _const_spec((1, dff)),
        _const_spec((dff, d)),
    ]
    if g_final is not None:
        args.append(g_final)
        in_specs.append(_const_spec((1, d)))
    return pl.pallas_call(
        kern,
        grid=(groups, rows // tm),
        in_specs=in_specs,
        out_specs=[
            row_spec,
            pl.BlockSpec((1, 2 * stride, dff), lambda gi, t: (gi, 0, 0)),
        ],
        out_shape=[
            jax.ShapeDtypeStruct((groups, rows, d), F32),
            jax.ShapeDtypeStruct((groups, 2 * stride, dff), F32),
        ],
        scratch_shapes=[pltpu.VMEM((pad + tm, dff), F32)],
        compiler_params=pltpu.CompilerParams(
            dimension_semantics=("arbitrary", "arbitrary"), vmem_limit_bytes=VMEM_LIMIT),
        name="ffn",
    )(*args)


def _qkv_kernel(x_ref, g_ref, w_ref, q_ref, k_ref, v_ref, kb_ref, vb_ref, *, d, scale):
    h = _rmsnorm(x_ref[...], g_ref[...]).astype(BF16)
    q_ref[...] = (_dot(h, w_ref[:, 0:d]) * scale).astype(BF16)
    k = _dot(h, w_ref[:, d:2 * d])
    k_ref[...] = k
    kb_ref[...] = k.astype(BF16)
    v = _dot(h, w_ref[:, 2 * d:3 * d])
    v_ref[...] = v
    vb_ref[...] = v.astype(BF16)


def _qkv(x, g, w, *, tm, scale):
    rows, d = x.shape
    row_spec = pl.BlockSpec((tm, d), lambda t: (t, 0))
    return pl.pallas_call(
        functools.partial(_qkv_kernel, d=d, scale=scale),
        grid=(rows // tm,),
        in_specs=[row_spec, _const_spec((1, d)), _const_spec((d, 3 * d))],
        out_specs=[row_spec] * 5,
        out_shape=[
            jax.ShapeDtypeStruct((rows, d), BF16),
            jax.ShapeDtypeStruct((rows, d), F32),
            jax.ShapeDtypeStruct((rows, d), F32),
            jax.ShapeDtypeStruct((rows, d), BF16),
            jax.ShapeDtypeStruct((rows, d), BF16),
        ],
        compiler_params=pltpu.CompilerParams(
            dimension_semantics=("arbitrary",), vmem_limit_bytes=VMEM_LIMIT),
        name="qkv",
    )(x, g, w)


def _sb_block(z, tri, mask):
    sp = jnp.maximum(z, 0.0) + jnp.log(1.0 + jnp.exp(-jnp.abs(z)))
    log_beta = z - sp
    if mask is not None:
        sp = jnp.where(mask, sp, 0.0)
    hi = sp.astype(BF16)
    lo = (sp - hi.astype(F32)).astype(BF16)
    suffix = _dot(hi, tri) + _dot(lo, tri)
    w = jnp.exp(log_beta - suffix)
    if mask is not None:
        w = jnp.where(mask, w, 0.0)
    return w, suffix[:, 0:1] + sp[:, 0:1]


def _attn_prompt_kernel(bias_ref, q_ref, k_ref, v_ref, tri_ref, o_ref, acc_ref, *, tq, dh):
    hp = pl.program_id(1)
    i = pl.program_id(2)
    first_head = lax.broadcasted_iota(jnp.int32, (1, LANES), 1) < dh
    q = q_ref[0]
    zero = jnp.zeros_like(q)
    q_heads = (jnp.where(first_head, q, zero), jnp.where(first_head, zero, q))
    bias = (bias_ref[2 * hp], bias_ref[2 * hp + 1])
    acc_ref[...] = jnp.zeros_like(acc_ref)

    def key_block(j, carry, mask):
        start = pl.multiple_of(j * tq, tq)
        kb = k_ref[0, pl.ds(start, tq), :]
        vb = v_ref[0, pl.ds(start, tq), :]
        out = []
        for h in range(2):
            z = lax.dot_general(q_heads[h], kb, _NT, preferred_element_type=F32) + bias[h]
            w, total = _sb_block(z, tri_ref[...], mask)
            pv = _dot(w.astype(BF16), vb)
            acc_ref[h] = acc_ref[h] + jnp.exp(-carry[h]) * pv
            out.append(carry[h] + total)
        return tuple(out)

    t_idx = lax.broadcasted_iota(jnp.int32, (tq, tq), 0)
    s_idx = lax.broadcasted_iota(jnp.int32, (tq, tq), 1)
    c0 = jnp.zeros((tq, 1), F32)
    carry = key_block(i, (c0, c0), s_idx < t_idx)
    lax.fori_loop(0, i, lambda jj, c: key_block(i - 1 - jj, c, None), carry)
    o_ref[0] = jnp.where(first_head, acc_ref[0], acc_ref[1]).astype(BF16)


def _attn_prompt(q, kb, vb, bias, tri, *, tq):
    batch, seq, d = q.shape
    dh = d // N_HEADS
    grid_spec = pltpu.PrefetchScalarGridSpec(
        num_scalar_prefetch=1,
        grid=(batch, d // LANES, seq // tq),
        in_specs=[
            pl.BlockSpec((1, tq, LANES), lambda b, hp, i, _: (b, i, hp)),
            pl.BlockSpec((1, seq, LANES), lambda b, hp, i, _: (b, 0, hp)),
            pl.BlockSpec((1, seq, LANES), lambda b, hp, i, _: (b, 0, hp)),
            pl.BlockSpec((tq, tq), lambda b, hp, i, _: (0, 0)),
        ],
        out_specs=pl.BlockSpec((1, tq, LANES), lambda b, hp, i, _: (b, i, hp)),
        scratch_shapes=[pltpu.VMEM((2, tq, LANES), F32)],
    )
    return pl.pallas_call(
        functools.partial(_attn_prompt_kernel, tq=tq, dh=dh),
        grid_spec=grid_spec,
        out_shape=jax.ShapeDtypeStruct((batch, seq, d), BF16),
        compiler_params=pltpu.CompilerParams(
            dimension_semantics=("arbitrary",) * 3, vmem_limit_bytes=VMEM_LIMIT),
        name="attn_prompt",
    )(bias, q, kb, vb, tri)


def _attn_sample_kernel(pt_ref, q_ref, kn_ref, vn_ref, kc_ref, vc_ref, bias_ref, tri_ref,
                        o_ref, qbd_ref, acc_new_ref, acc_t_ref, c_ref, *, page, dh):
    del pt_ref
    p = pl.program_id(1)
    rows, d = acc_new_ref.shape

    @pl.when(p == 0)
    def _():
        q_all = jnp.broadcast_to(q_ref[0][None], (N_HEADS, SUBLANES, d)).reshape(rows, d)
        row_head = lax.div(lax.broadcasted_iota(jnp.int32, (rows, d), 0), SUBLANES)
        col_head = lax.div(lax.broadcasted_iota(jnp.int32, (rows, d), 1), dh)
        qbd = jnp.where(row_head == col_head, q_all, 0.0).astype(BF16)
        qbd_ref[...] = qbd
        acc_t_ref[...] = jnp.zeros_like(acc_t_ref)
        fill = jnp.zeros((page - SUBLANES, d), F32)
        k_blk = jnp.concatenate([kn_ref[0], fill], axis=0).astype(BF16)
        v_blk = jnp.concatenate([vn_ref[0], fill], axis=0).astype(BF16)
        t_idx = lax.rem(lax.broadcasted_iota(jnp.int32, (rows, page), 0), SUBLANES)
        s_idx = lax.broadcasted_iota(jnp.int32, (rows, page), 1)
        z = lax.dot_general(qbd, k_blk, _NT, preferred_element_type=F32) + bias_ref[...]
        w, total = _sb_block(z, tri_ref[...], s_idx < t_idx)
        acc_new_ref[...] = _dot(w.astype(BF16), v_blk)
        c_ref[...] = total

    @pl.when(p > 0)
    def _():
        z = _dot(qbd_ref[...], kc_ref[0].astype(BF16)) + bias_ref[...]
        w, total = _sb_block(z, tri_ref[...], None)
        w = (w * jnp.exp(-c_ref[...])).astype(BF16)
        acc_t_ref[...] = acc_t_ref[...] + lax.dot_general(
            vc_ref[0].astype(BF16), w, _NT, preferred_element_type=F32)
        c_ref[...] = c_ref[...] + total

    @pl.when(p == pl.num_programs(1) - 1)
    def _():
        acc = acc_new_ref[...] + acc_t_ref[...].T
        col_head = lax.div(lax.broadcasted_iota(jnp.int32, (SUBLANES, d), 1), dh)
        out = jnp.zeros((SUBLANES, d), F32)
        for h in range(N_HEADS):
            out = jnp.where(col_head == h, acc[h * SUBLANES:(h + 1) * SUBLANES, :], out)
        o_ref[0] = out


def _attn_sample(page_table, q, k_new, v_new, cache_kt, cache_vt, bias_rows, tri):
    n_seq, n_pages = page_table.shape
    _, d, page = cache_kt.shape
    dh = d // N_HEADS
    rows = N_HEADS * SUBLANES

    def cache_map(s, p, pt):
        logical = n_pages - 1 - jnp.maximum(p - 1, 0)
        return (pt[s * n_pages + logical], 0, 0)

    seq_spec = pl.BlockSpec((1, SUBLANES, d), lambda s, p, pt: (s, 0, 0))
    grid_spec = pltpu.PrefetchScalarGridSpec(
        num_scalar_prefetch=1,
        grid=(n_seq, n_pages + 1),
        in_specs=[
            seq_spec, seq_spec, seq_spec,
            pl.BlockSpec((1, d, page), cache_map),
            pl.BlockSpec((1, d, page), cache_map),
            pl.BlockSpec((rows, 1), lambda s, p, pt: (0, 0)),
            pl.BlockSpec((page, page), lambda s, p, pt: (0, 0)),
        ],
        out_specs=seq_spec,
        scratch_shapes=[
            pltpu.VMEM((rows, d), BF16),
            pltpu.VMEM((rows, d), F32),
            pltpu.VMEM((d, rows), F32),
            pltpu.VMEM((rows, 1), F32),
        ],
    )
    return pl.pallas_call(
        functools.partial(_attn_sample_kernel, page=page, dh=dh),
        grid_spec=grid_spec,
        out_shape=jax.ShapeDtypeStruct((n_seq, SUBLANES, d), F32),
        compiler_params=pltpu.CompilerParams(
            dimension_semantics=("arbitrary", "arbitrary"), vmem_limit_bytes=VMEM_LIMIT),
        name="attn_sample",
    )(page_table.reshape(-1), q, k_new, v_new, cache_kt, cache_vt, bias_rows, tri)


def _strict_lower(n):
    r = lax.broadcasted_iota(jnp.int32, (n, n), 0)
    c = lax.broadcasted_iota(jnp.int32, (n, n), 1)
    return (r > c).astype(BF16)


def kernel(x_prompt, x_sample, cache_k, cache_v, state_conv, state_ffn, page_table, norm_mix, norm_ffn, norm_final, w_sc_in, w_sc_dw, w_sc_out, w_qkv, w_o, sb_bias, w_ffn_up, w_ffn_dw, b_ffn_dw, w_ffn_down):
    bp, tp, d = x_prompt.shape
    bs, ts, _ = x_sample.shape
    dff = w_ffn_down.shape[1]
    dh = d // N_HEADS
    page = cache_k.shape[2]
    hist = CONV_WIDTH - 1
    scale = 1.0 / math.sqrt(dh)

    w_in, w_out = w_sc_in[0].astype(BF16), w_sc_out[0].astype(BF16)
    wqkv, wo = w_qkv[0].astype(BF16), w_o[0].astype(BF16)
    w_up, w_down = w_ffn_up.astype(BF16), w_ffn_down.astype(BF16)
    g_mix = norm_mix[:, None, :]
    g_ffn = norm_ffn[:, None, :]
    g_fin = norm_final[None, :]
    b_dw = b_ffn_dw[:, None, :]
    bias = sb_bias[0]

    def trunk(x, conv_prev, ffn_prev, attend, *, stride, tm_mix, tm_ffn, tm_qkv):
        groups, rows, _ = x.shape
        x, conv_new = _mixer0(x, conv_prev, g_mix[0], w_in, w_sc_dw[0], w_out,
                              stride=stride, tm=tm_mix)
        x, ffn_new0 = _ffn(x, ffn_prev[0], g_ffn[0], w_up[0], w_ffn_dw[0], b_dw[0], w_down[0],
                           stride=stride, tm=tm_ffn)
        q, k, v, kb, vb = _qkv(x.reshape(groups * rows, d), g_mix[1], wqkv, tm=tm_qkv, scale=scale)
        o = attend(q, k, v, kb, vb)
        y, ffn_new1 = _ffn(x, ffn_prev[1], g_ffn[1], w_up[1], w_ffn_dw[1], b_dw[1], w_down[1],
                           stride=stride, tm=tm_ffn, oproj=(o.reshape(groups, rows, d), wo),
                           g_final=g_fin)
        return y, conv_new, (ffn_new0, ffn_new1), k, v

    tq = 256

    def attend_prompt(q, k, v, kb, vb):
        shp = (bp, tp, d)
        return _attn_prompt(q.reshape(shp), kb.reshape(shp), vb.reshape(shp), bias,
                            _strict_lower(tq), tq=tq)

    y_p, conv_p, ffn_p, k_p, v_p = trunk(
        x_prompt, jnp.zeros((bp, hist, d), F32), jnp.zeros((2, bp, hist, dff), F32),
        attend_prompt, stride=1, tm_mix=512, tm_ffn=256, tm_qkv=512)

    def to_rows(a):
        return a.transpose(1, 0, 2).reshape(1, a.shape[1] * bs, a.shape[2])

    def from_rows(a, t):
        return a.reshape(t, bs, a.shape[-1]).transpose(1, 0, 2)

    def per_seq_tile(a):
        return jnp.pad(from_rows(a, ts).astype(F32), ((0, 0), (0, SUBLANES - ts), (0, 0)))

    def pages_t(cache):
        return cache.transpose(0, 2, 3, 1).reshape(cache.shape[0], d, page)

    def attend_sample(q, k, v, kb, vb):
        bias_rows = jnp.repeat(bias, SUBLANES)[:, None]
        o = _attn_sample(page_table, per_seq_tile(q), per_seq_tile(k), per_seq_tile(v),
                         pages_t(cache_k[0]), pages_t(cache_v[0]),
                         bias_rows, _strict_lower(page))
        return to_rows(o[:, :ts]).astype(BF16)

    y_s, conv_s, ffn_s, k_s, v_s = trunk(
        to_rows(x_sample), to_rows(state_conv[0]), jnp.stack([to_rows(state_ffn[0]), to_rows(state_ffn[1])]),
        attend_sample, stride=bs, tm_mix=ts * bs, tm_ffn=ts * bs, tm_qkv=ts * bs)

    heads = (N_HEADS, dh)
    return (
        y_p,
        from_rows(y_s[0], ts),
        k_p.reshape(1, bp, tp, *heads),
        v_p.reshape(1, bp, tp, *heads),
        conv_p[None],
        jnp.stack(ffn_p),
        from_rows(k_s, ts).reshape(1, bs, ts, *heads),
        from_rows(v_s, ts).reshape(1, bs, ts, *heads),
        from_rows(conv_s[0], hist)[None],
        jnp.stack([from_rows(f[0], hist) for f in ffn_s]),
    )
```

```python
import functools
import math

import jax
import jax.numpy as jnp
from jax import lax
from jax.experimental import pallas as pl
from jax.experimental.pallas import tpu as pltpu

F32 = jnp.float32
BF16 = jnp.bfloat16

NORM_EPS = 1e-6
CONV_WIDTH = 3
N_HEADS = 16
SUBLANES = 8
LANES = 128
VMEM_LIMIT = 56 * 1024 * 1024
LOG2_E = math.log2(math.e)
MASKED = -1e30
SOFTPLUS2_THRESHOLD = 64.0
PAGES_PER_STEP = 8

_NT = (((1,), (1,)), ((), ()))


def _rmsnorm(x, g):
    ms = jnp.mean(x * x, axis=-1, keepdims=True)
    return x * lax.rsqrt(ms + NORM_EPS) * g


def _dot(a, b):
    return jnp.dot(a, b, preferred_element_type=F32)


def _const_spec(shape):
    return pl.BlockSpec(shape, lambda *_: (0,) * len(shape), pipeline_mode=pl.Buffered(1))


def _conv_rows(ext_ref, cur, w_ref, *, stride, pad, tm):
    ext_ref[pad:pad + tm, :] = cur
    lo = pad - 2 * stride
    y = ext_ref[lo:lo + tm, :] * w_ref[0:1, :]
    y = y + ext_ref[lo + stride:lo + stride + tm, :] * w_ref[1:2, :]
    return y + cur * w_ref[2:3, :]


def _roll_history(ext_ref, st_ref, *, stride, pad, tm):
    hist = ext_ref[pad + tm - 2 * stride:pad + tm, :]
    st_ref[0] = hist
    ext_ref[pad - 2 * stride:pad, :] = hist


def _mixer0_kernel(x_ref, prev_ref, g_ref, win_ref, wdw_ref, wout_ref,
                   y_ref, st_ref, ext_ref, *, stride, pad, tm, d):
    @pl.when(pl.program_id(1) == 0)
    def _():
        ext_ref[pad - 2 * stride:pad, :] = prev_ref[0]

    x = x_ref[0]
    h = _rmsnorm(x, g_ref[...]).astype(BF16)
    b_gate = _dot(h, win_ref[:, 0:d])
    u = _dot(h, win_ref[:, d:2 * d]) * _dot(h, win_ref[:, 2 * d:3 * d])
    uc = _conv_rows(ext_ref, u, wdw_ref, stride=stride, pad=pad, tm=tm)
    y_ref[0] = x + _dot((b_gate * uc).astype(BF16), wout_ref[...])
    _roll_history(ext_ref, st_ref, stride=stride, pad=pad, tm=tm)


def _mixer0(x, prev, g, w_in, w_dw, w_out, *, stride, tm):
    groups, rows, d = x.shape
    pad = -(-2 * stride // SUBLANES) * SUBLANES
    kern = functools.partial(_mixer0_kernel, stride=stride, pad=pad, tm=tm, d=d)
    return pl.pallas_call(
        kern,
        grid=(groups, rows // tm),
        in_specs=[
            pl.BlockSpec((1, tm, d), lambda gi, t: (gi, t, 0)),
            pl.BlockSpec((1, 2 * stride, d), lambda gi, t: (gi, 0, 0)),
            _const_spec((1, d)),
            _const_spec((d, 3 * d)),
            _const_spec((CONV_WIDTH, d)),
            _const_spec((d, d)),
        ],
        out_specs=[
            pl.BlockSpec((1, tm, d), lambda gi, t: (gi, t, 0)),
            pl.BlockSpec((1, 2 * stride, d), lambda gi, t: (gi, 0, 0)),
        ],
        out_shape=[
            jax.ShapeDtypeStruct((groups, rows, d), F32),
            jax.ShapeDtypeStruct((groups, 2 * stride, d), F32),
        ],
        scratch_shapes=[pltpu.VMEM((pad + tm, d), F32)],
        compiler_params=pltpu.CompilerParams(
            dimension_semantics=("arbitrary", "arbitrary"), vmem_limit_bytes=VMEM_LIMIT),
        name="mixer0",
    )(x, prev, g, w_in, w_dw, w_out)


def _ffn_kernel(*refs, stride, pad, tm, dff, oproj, final):
    refs = list(refs)
    x_ref = refs.pop(0)
    if oproj:
        o_ref, wo_ref = refs.pop(0), refs.pop(0)
    prev_ref, g_ref, wup_ref, wdw_ref, bdw_ref, wdown_ref = refs[:6]
    refs = refs[6:]
    if final:
        gf_ref = refs.pop(0)
    y_ref, st_ref, ext_ref = refs

    @pl.when(pl.program_id(1) == 0)
    def _():
        ext_ref[pad - 2 * stride:pad, :] = prev_ref[0]

    x = x_ref[0]
    if oproj:
        x = x + _dot(o_ref[0], wo_ref[...])
    h = _rmsnorm(x, g_ref[...]).astype(BF16)
    gate = _dot(h, wup_ref[:, 0:dff])
    up = _dot(h, wup_ref[:, dff:2 * dff])
    gc = _conv_rows(ext_ref, gate, wdw_ref, stride=stride, pad=pad, tm=tm) + bdw_ref[...]
    act = gc / (1.0 + jnp.exp(-gc)) * up
    xo = x + _dot(act.astype(BF16), wdown_ref[...])
    y_ref[0] = _rmsnorm(xo, gf_ref[...]) if final else xo
    _roll_history(ext_ref, st_ref, stride=stride, pad=pad, tm=tm)


def _ffn(x, prev, g, w_up, w_dw, b_dw, w_down, *, stride, tm, oproj=None, g_final=None):
    groups, rows, d = x.shape
    dff = w_down.shape[0]
    pad = -(-2 * stride // SUBLANES) * SUBLANES
    kern = functools.partial(_ffn_kernel, stride=stride, pad=pad, tm=tm, dff=dff,
                             oproj=oproj is not None, final=g_final is not None)
    row_spec = pl.BlockSpec((1, tm, d), lambda gi, t: (gi, t, 0))
    args, in_specs = [x], [row_spec]
    if oproj is not None:
        args += list(oproj)
        in_specs += [row_spec, _const_spec((d, d))]
    args += [prev, g, w_up, w_dw, b_dw, w_down]
    in_specs += [
        pl.BlockSpec((1, 2 * stride, dff), lambda gi, t: (gi, 0, 0)),
        _const_spec((1, d)),
        _const_spec((d, 2 * dff)),
        _const_spec((CONV_WIDTH, dff)),
        _const_spec((1, dff)),
        _const_spec((dff, d)),
    ]
    if g_final is not None:
        args.append(g_final)
        in_specs.append(_const_spec((1, d)))
    return pl.pallas_call(
        kern,
        grid=(groups, rows // tm),
        in_specs=in_specs,
        out_specs=[
            row_spec,
            pl.BlockSpec((1, 2 * stride, dff), lambda gi, t: (gi, 0, 0)),
        ],
        out_shape=[
            jax.ShapeDtypeStruct((groups, rows, d), F32),
            jax.ShapeDtypeStruct((groups, 2 * stride, dff), F32),
        ],
        scratch_shapes=[pltpu.VMEM((pad + tm, dff), F32)],
        compiler_params=pltpu.CompilerParams(
            dimension_semantics=("arbitrary", "arbitrary"), vmem_limit_bytes=VMEM_LIMIT),
        name="ffn",
    )(*args)


def _qkv_kernel(x_ref, g_ref, w_ref, q_ref, k_ref, v_ref, kb_ref, vb_ref, *, d, scale):
    h = _rmsnorm(x_ref[...], g_ref[...]).astype(BF16)
    q_ref[...] = (_dot(h, w_ref[:, 0:d]) * scale).astype(BF16)
    k = _dot(h, w_ref[:, d:2 * d])
    k_ref[...] = k
    kb_ref[...] = k.astype(BF16)
    v = _dot(h, w_ref[:, 2 * d:3 * d])
    v_ref[...] = v
    vb_ref[...] = v.astype(BF16)


def _qkv(x, g, w, *, tm, scale):
    rows, d = x.shape
    row_spec = pl.BlockSpec((tm, d), lambda t: (t, 0))
    return pl.pallas_call(
        functools.partial(_qkv_kernel, d=d, scale=scale),
        grid=(rows // tm,),
        in_specs=[row_spec, _const_spec((1, d)), _const_spec((d, 3 * d))],
        out_specs=[row_spec] * 5,
        out_shape=[
            jax.ShapeDtypeStruct((rows, d), BF16),
            jax.ShapeDtypeStruct((rows, d), F32),
            jax.ShapeDtypeStruct((rows, d), F32),
            jax.ShapeDtypeStruct((rows, d), BF16),
            jax.ShapeDtypeStruct((rows, d), BF16),
        ],
        compiler_params=pltpu.CompilerParams(
            dimension_semantics=("arbitrary",), vmem_limit_bytes=VMEM_LIMIT),
        name="qkv",
    )(x, g, w)


def _softplus2(z):
    return jnp.maximum(jnp.log2(1.0 + jnp.exp2(jnp.minimum(z, SOFTPLUS2_THRESHOLD))), z)


def _attn_prompt_kernel(bias_ref, q_ref, k_ref, v_ref, tri_ref, o_ref,
                        z_ref, suf_ref, acc_ref, *, tq, dh):
    n_heads = acc_ref.shape[0]
    first = pl.program_id(1) * n_heads
    i = pl.program_id(2)
    first_head = lax.broadcasted_iota(jnp.int32, (1, LANES), 1) < dh

    def lanes(h):
        return slice((h // 2) * LANES, (h // 2 + 1) * LANES)

    q_heads = []
    for h in range(n_heads):
        q = q_ref[0, :, lanes(h)]
        keep = first_head if h % 2 == 0 else jnp.logical_not(first_head)
        q_heads.append(jnp.where(keep, q, jnp.zeros_like(q)))
    bias = [bias_ref[first + h] for h in range(n_heads)]
    acc_ref[...] = jnp.zeros_like(acc_ref)

    def block_start(item):
        return pl.multiple_of((i - jnp.minimum(item, i)) * tq, tq)

    def scores(item, parity):
        rows = pl.ds(block_start(item), tq)
        for h in range(n_heads):
            z_ref[parity, h] = lax.dot_general(
                q_heads[h], k_ref[0, rows, lanes(h)], _NT, preferred_element_type=F32) + bias[h]

    def suffix(item, parity, later, mask=None):
        out = []
        for h in range(n_heads):
            z = z_ref[parity, h]
            if mask is not None:
                z = jnp.where(mask, z, MASKED)
                z_ref[parity, h] = z
            suf = _dot(_softplus2(z).astype(BF16), tri_ref[...])
            suf_ref[h] = suf + jnp.where(item > i, -MASKED, later[h])
            out.append(later[h] + suf[:, 0:1])
        return tuple(out)

    def weights(item, parity):
        rows = pl.ds(block_start(item), tq)
        for h in range(n_heads):
            w = jnp.exp2(z_ref[parity, h] - suf_ref[h]).astype(BF16)
            acc_ref[h] = acc_ref[h] + _dot(w, v_ref[0, rows, lanes(h)])

    t_idx = lax.broadcasted_iota(jnp.int32, (tq, tq), 0)
    s_idx = lax.broadcasted_iota(jnp.int32, (tq, tq), 1)
    none_later = jnp.zeros((tq, 1), F32)
    scores(0, 0)
    later = suffix(0, 0, (none_later,) * n_heads, s_idx < t_idx)
    scores(1, 1)

    def trip(m, later):
        n = 2 * m
        weights(n, 0)
        later = suffix(n + 1, 1, later)
        scores(n + 2, 0)
        weights(n + 1, 1)
        later = suffix(n + 2, 0, later)
        scores(n + 3, 1)
        return later

    lax.fori_loop(0, (i + 2) // 2, trip, later)
    for h in range(0, n_heads, 2):
        o_ref[0, :, lanes(h)] = jnp.where(first_head, acc_ref[h], acc_ref[h + 1]).astype(BF16)


def _attn_prompt(q, kb, vb, bias, tri, *, tq, heads_per_step):
    batch, seq, d = q.shape
    dh = d // N_HEADS
    width = heads_per_step * dh
    grid_spec = pltpu.PrefetchScalarGridSpec(
        num_scalar_prefetch=1,
        grid=(batch, d // width, seq // tq),
        in_specs=[
            pl.BlockSpec((1, tq, width), lambda b, hg, i, _: (b, i, hg)),
            pl.BlockSpec((1, seq, width), lambda b, hg, i, _: (b, 0, hg)),
            pl.BlockSpec((1, seq, width), lambda b, hg, i, _: (b, 0, hg)),
            pl.BlockSpec((tq, tq), lambda b, hg, i, _: (0, 0)),
        ],
        out_specs=pl.BlockSpec((1, tq, width), lambda b, hg, i, _: (b, i, hg)),
        scratch_shapes=[
            pltpu.VMEM((2, heads_per_step, tq, tq), F32),
            pltpu.VMEM((heads_per_step, tq, tq), F32),
            pltpu.VMEM((heads_per_step, tq, LANES), F32),
        ],
    )
    return pl.pallas_call(
        functools.partial(_attn_prompt_kernel, tq=tq, dh=dh),
        grid_spec=grid_spec,
        out_shape=jax.ShapeDtypeStruct((batch, seq, d), BF16),
        compiler_params=pltpu.CompilerParams(
            dimension_semantics=("arbitrary",) * 3, vmem_limit_bytes=VMEM_LIMIT),
        name="attn_prompt",
    )(bias, q, kb, vb, tri)


def _attn_sample_kernel(*refs, n_pp, page, dh):
    pt_ref, q_ref, kn_ref, vn_ref = refs[:4]
    kc_refs = refs[4:4 + n_pp]
    vc_refs = refs[4 + n_pp:4 + 2 * n_pp]
    bias_ref, tri_ref, o_ref, qbd_ref, acc_new_ref, acc_t_ref, later_ref = refs[4 + 2 * n_pp:]
    del pt_ref
    p = pl.program_id(1)
    rows, d = acc_new_ref.shape

    @pl.when(p == 0)
    def _():
        q_all = jnp.broadcast_to(q_ref[0][None], (N_HEADS, SUBLANES, d)).reshape(rows, d)
        row_head = lax.div(lax.broadcasted_iota(jnp.int32, (rows, d), 0), SUBLANES)
        col_head = lax.div(lax.broadcasted_iota(jnp.int32, (rows, d), 1), dh)
        qbd = jnp.where(row_head == col_head, q_all, 0.0).astype(BF16)
        qbd_ref[...] = qbd
        acc_t_ref[...] = jnp.zeros_like(acc_t_ref)
        fill = jnp.zeros((page - SUBLANES, d), F32)
        k_blk = jnp.concatenate([kn_ref[0], fill], axis=0).astype(BF16)
        v_blk = jnp.concatenate([vn_ref[0], fill], axis=0).astype(BF16)
        t_idx = lax.rem(lax.broadcasted_iota(jnp.int32, (rows, page), 0), SUBLANES)
        s_idx = lax.broadcasted_iota(jnp.int32, (rows, page), 1)
        z = lax.dot_general(qbd, k_blk, _NT, preferred_element_type=F32) + bias_ref[...]
        z = jnp.where(s_idx < t_idx, z, MASKED)
        suf = _dot(_softplus2(z).astype(BF16), tri_ref[0:page, 0:page])
        acc_new_ref[...] = _dot(jnp.exp2(z - suf).astype(BF16), v_blk)
        later_ref[...] = suf[:, 0:1]

    later = later_ref[...]
    contrib = None
    for b in range(n_pp // 2):
        kt = jnp.concatenate([kc_refs[2 * b + 1][0], kc_refs[2 * b][0]], axis=1).astype(BF16)
        vt = jnp.concatenate([vc_refs[2 * b + 1][0], vc_refs[2 * b][0]], axis=1).astype(BF16)
        z = _dot(qbd_ref[...], kt) + bias_ref[...]
        suf = _dot(_softplus2(z).astype(BF16), tri_ref[...])
        w = jnp.exp2(z - (suf + later)).astype(BF16)
        pv = lax.dot_general(vt, w, _NT, preferred_element_type=F32)
        contrib = pv if contrib is None else contrib + pv
        later = later + suf[:, 0:1]
    acc_t_ref[...] = acc_t_ref[...] + contrib
    later_ref[...] = later

    @pl.when(p == pl.num_programs(1) - 1)
    def _():
        acc = acc_new_ref[...] + acc_t_ref[...].T
        col_head = lax.div(lax.broadcasted_iota(jnp.int32, (SUBLANES, d), 1), dh)
        out = jnp.zeros((SUBLANES, d), F32)
        for h in range(N_HEADS):
            out = jnp.where(col_head == h, acc[h * SUBLANES:(h + 1) * SUBLANES, :], out)
        o_ref[0] = out


def _attn_sample(page_table, q, k_new, v_new, cache_kt, cache_vt, bias_rows, tri):
    n_seq, n_pages = page_table.shape
    _, d, page = cache_kt.shape
    dh = d // N_HEADS
    rows = N_HEADS * SUBLANES
    n_pp = PAGES_PER_STEP

    def cache_spec(slot):
        def index_map(s, p, pt):
            return (pt[s * n_pages + n_pages - 1 - (p * n_pp + slot)], 0, 0)
        return pl.BlockSpec((1, d, page), index_map)

    seq_spec = pl.BlockSpec((1, SUBLANES, d), lambda s, p, pt: (s, 0, 0))
    cache_specs = [cache_spec(slot) for slot in range(n_pp)]
    grid_spec = pltpu.PrefetchScalarGridSpec(
        num_scalar_prefetch=1,
        grid=(n_seq, n_pages // n_pp),
        in_specs=[seq_spec, seq_spec, seq_spec] + cache_specs + cache_specs + [
            pl.BlockSpec((rows, 1), lambda s, p, pt: (0, 0)),
            pl.BlockSpec((2 * page, 2 * page), lambda s, p, pt: (0, 0)),
        ],
        out_specs=seq_spec,
        scratch_shapes=[
            pltpu.VMEM((rows, d), BF16),
            pltpu.VMEM((rows, d), F32),
            pltpu.VMEM((d, rows), F32),
            pltpu.VMEM((rows, 1), F32),
        ],
    )
    return pl.pallas_call(
        functools.partial(_attn_sample_kernel, n_pp=n_pp, page=page, dh=dh),
        grid_spec=grid_spec,
        out_shape=jax.ShapeDtypeStruct((n_seq, SUBLANES, d), F32),
        compiler_params=pltpu.CompilerParams(
            dimension_semantics=("arbitrary", "arbitrary"), vmem_limit_bytes=VMEM_LIMIT),
        name="attn_sample",
    )(page_table.reshape(-1), q, k_new, v_new, *([cache_kt] * n_pp), *([cache_vt] * n_pp),
      bias_rows, tri)


def _suffix_ones(n):
    r = lax.broadcasted_iota(jnp.int32, (n, n), 0)
    c = lax.broadcasted_iota(jnp.int32, (n, n), 1)
    return (r >= c).astype(BF16)


def kernel(x_prompt, x_sample, cache_k, cache_v, state_conv, state_ffn, page_table, norm_mix, norm_ffn, norm_final, w_sc_in, w_sc_dw, w_sc_out, w_qkv, w_o, sb_bias, w_ffn_up, w_ffn_dw, b_ffn_dw, w_ffn_down):
    bp, tp, d = x_prompt.shape
    bs, ts, _ = x_sample.shape
    dff = w_ffn_down.shape[1]
    dh = d // N_HEADS
    page = cache_k.shape[2]
    hist = CONV_WIDTH - 1
    scale = LOG2_E / math.sqrt(dh)

    w_in, w_out = w_sc_in[0].astype(BF16), w_sc_out[0].astype(BF16)
    wqkv, wo = w_qkv[0].astype(BF16), w_o[0].astype(BF16)
    w_up, w_down = w_ffn_up.astype(BF16), w_ffn_down.astype(BF16)
    g_mix = norm_mix[:, None, :]
    g_ffn = norm_ffn[:, None, :]
    g_fin = norm_final[None, :]
    b_dw = b_ffn_dw[:, None, :]
    bias = sb_bias[0] * LOG2_E

    def trunk(x, conv_prev, ffn_prev, attend, *, stride, tm_mix, tm_ffn, tm_qkv):
        groups, rows, _ = x.shape
        x, conv_new = _mixer0(x, conv_prev, g_mix[0], w_in, w_sc_dw[0], w_out,
                              stride=stride, tm=tm_mix)
        x, ffn_new0 = _ffn(x, ffn_prev[0], g_ffn[0], w_up[0], w_ffn_dw[0], b_dw[0], w_down[0],
                           stride=stride, tm=tm_ffn)
        q, k, v, kb, vb = _qkv(x.reshape(groups * rows, d), g_mix[1], wqkv, tm=tm_qkv, scale=scale)
        o = attend(q, k, v, kb, vb)
        y, ffn_new1 = _ffn(x, ffn_prev[1], g_ffn[1], w_up[1], w_ffn_dw[1], b_dw[1], w_down[1],
                           stride=stride, tm=tm_ffn, oproj=(o.reshape(groups, rows, d), wo),
                           g_final=g_fin)
        return y, conv_new, (ffn_new0, ffn_new1), k, v

    tq = 256

    def attend_prompt(q, k, v, kb, vb):
        shp = (bp, tp, d)
        return _attn_prompt(q.reshape(shp), kb.reshape(shp), vb.reshape(shp), bias,
                            _suffix_ones(tq), tq=tq, heads_per_step=4)

    y_p, conv_p, ffn_p, k_p, v_p = trunk(
        x_prompt, jnp.zeros((bp, hist, d), F32), jnp.zeros((2, bp, hist, dff), F32),
        attend_prompt, stride=1, tm_mix=512, tm_ffn=256, tm_qkv=512)

    def to_rows(a):
        return a.transpose(1, 0, 2).reshape(1, a.shape[1] * bs, a.shape[2])

    def from_rows(a, t):
        return a.reshape(t, bs, a.shape[-1]).transpose(1, 0, 2)

    def per_seq_tile(a):
        return jnp.pad(from_rows(a, ts).astype(F32), ((0, 0), (0, SUBLANES - ts), (0, 0)))

    def pages_t(cache):
        return cache.transpose(0, 2, 3, 1).reshape(cache.shape[0], d, page)

    def attend_sample(q, k, v, kb, vb):
        bias_rows = jnp.repeat(bias, SUBLANES)[:, None]
        o = _attn_sample(page_table, per_seq_tile(q), per_seq_tile(k), per_seq_tile(v),
                         pages_t(cache_k[0]), pages_t(cache_v[0]),
                         bias_rows, _suffix_ones(2 * page))
        return to_rows(o[:, :ts]).astype(BF16)

    y_s, conv_s, ffn_s, k_s, v_s = trunk(
        to_rows(x_sample), to_rows(state_conv[0]), jnp.stack([to_rows(state_ffn[0]), to_rows(state_ffn[1])]),
        attend_sample, stride=bs, tm_mix=ts * bs, tm_ffn=ts * bs, tm_qkv=ts * bs)

    heads = (N_HEADS, dh)
    return (
        y_p,
        from_rows(y_s[0], ts),
        k_p.reshape(1, bp, tp, *heads),
        v_p.reshape(1, bp, tp, *heads),
        conv_p[None],
        jnp.stack(ffn_p),
        from_rows(k_s, ts).reshape(1, bs, ts, *heads),
        from_rows(v_s, ts).reshape(1, bs, ts, *heads),
        from_rows(conv_s[0], hist)[None],
        jnp.stack([from_rows(f[0], hist) for f in ffn_s]),
    )
```

```python
import functools
import math

import jax
import jax.numpy as jnp
from jax import lax
from jax.experimental import pallas as pl
from jax.experimental.pallas import tpu as pltpu

F32 = jnp.float32
BF16 = jnp.bfloat16

NORM_EPS = 1e-6
CONV_WIDTH = 3
N_HEADS = 16
SUBLANES = 8
LANES = 128
VMEM_LIMIT = 56 * 1024 * 1024
LOG2_E = math.log2(math.e)
MASKED = -1e30
SOFTPLUS2_THRESHOLD = 64.0
BIAS_PIECES = 3
PAGES_PER_STEP = 8
KEY_BLOCK = 256

_NT = (((1,), (1,)), ((), ()))


def _rmsnorm(x, g):
    ms = jnp.mean(x * x, axis=-1, keepdims=True)
    return x * lax.rsqrt(ms + NORM_EPS) * g


def _dot(a, b):
    return jnp.dot(a, b, preferred_element_type=F32)


def _const_spec(shape):
    return pl.BlockSpec(shape, lambda *_: (0,) * len(shape), pipeline_mode=pl.Buffered(1))


def _conv_rows(ext_ref, cur, w_ref, *, stride, pad, tm):
    ext_ref[pad:pad + tm, :] = cur
    lo = pad - 2 * stride
    y = ext_ref[lo:lo + tm, :] * w_ref[0:1, :]
    y = y + ext_ref[lo + stride:lo + stride + tm, :] * w_ref[1:2, :]
    return y + cur * w_ref[2:3, :]


def _roll_history(ext_ref, st_ref, *, stride, pad, tm):
    hist = ext_ref[pad + tm - 2 * stride:pad + tm, :]
    st_ref[0] = hist
    ext_ref[pad - 2 * stride:pad, :] = hist


def _mixer0_kernel(x_ref, prev_ref, g_ref, win_ref, wdw_ref, wout_ref,
                   y_ref, st_ref, ext_ref, *, stride, pad, tm, d):
    @pl.when(pl.program_id(1) == 0)
    def _():
        ext_ref[pad - 2 * stride:pad, :] = prev_ref[0]

    x = x_ref[0]
    h = _rmsnorm(x, g_ref[...]).astype(BF16)
    b_gate = _dot(h, win_ref[:, 0:d])
    u = _dot(h, win_ref[:, d:2 * d]) * _dot(h, win_ref[:, 2 * d:3 * d])
    uc = _conv_rows(ext_ref, u, wdw_ref, stride=stride, pad=pad, tm=tm)
    y_ref[0] = x + _dot((b_gate * uc).astype(BF16), wout_ref[...])
    _roll_history(ext_ref, st_ref, stride=stride, pad=pad, tm=tm)


def _mixer0(x, prev, g, w_in, w_dw, w_out, *, stride, tm):
    groups, rows, d = x.shape
    pad = -(-2 * stride // SUBLANES) * SUBLANES
    kern = functools.partial(_mixer0_kernel, stride=stride, pad=pad, tm=tm, d=d)
    return pl.pallas_call(
        kern,
        grid=(groups, rows // tm),
        in_specs=[
            pl.BlockSpec((1, tm, d), lambda gi, t: (gi, t, 0)),
            pl.BlockSpec((1, 2 * stride, d), lambda gi, t: (gi, 0, 0)),
            _const_spec((1, d)),
            _const_spec((d, 3 * d)),
            _const_spec((CONV_WIDTH, d)),
            _const_spec((d, d)),
        ],
        out_specs=[
            pl.BlockSpec((1, tm, d), lambda gi, t: (gi, t, 0)),
            pl.BlockSpec((1, 2 * stride, d), lambda gi, t: (gi, 0, 0)),
        ],
        out_shape=[
            jax.ShapeDtypeStruct((groups, rows, d), F32),
            jax.ShapeDtypeStruct((groups, 2 * stride, d), F32),
        ],
        scratch_shapes=[pltpu.VMEM((pad + tm, d), F32)],
        compiler_params=pltpu.CompilerParams(
            dimension_semantics=("arbitrary", "arbitrary"), vmem_limit_bytes=VMEM_LIMIT),
        name="mixer0",
    )(x, prev, g, w_in, w_dw, w_out)


def _ffn_kernel(*refs, stride, pad, tm, dff, oproj, final):
    refs = list(refs)
    x_ref = refs.pop(0)
    if oproj:
        o_ref, wo_ref = refs.pop(0), refs.pop(0)
    prev_ref, g_ref, wup_ref, wdw_ref, bdw_ref, wdown_ref = refs[:6]
    refs = refs[6:]
    if final:
        gf_ref = refs.pop(0)
    y_ref, st_ref, ext_ref = refs

    @pl.when(pl.program_id(1) == 0)
    def _():
        ext_ref[pad - 2 * stride:pad, :] = prev_ref[0]

    x = x_ref[0]
    if oproj:
        x = x + _dot(o_ref[0], wo_ref[...])
    h = _rmsnorm(x, g_ref[...]).astype(BF16)
    gate = _dot(h, wup_ref[:, 0:dff])
    up = _dot(h, wup_ref[:, dff:2 * dff])
    gc = _conv_rows(ext_ref, gate, wdw_ref, stride=stride, pad=pad, tm=tm) + bdw_ref[...]
    act = gc / (1.0 + jnp.exp(-gc)) * up
    xo = x + _dot(act.astype(BF16), wdown_ref[...])
    y_ref[0] = _rmsnorm(xo, gf_ref[...]) if final else xo
    _roll_history(ext_ref, st_ref, stride=stride, pad=pad, tm=tm)


def _ffn(x, prev, g, w_up, w_dw, b_dw, w_down, *, stride, tm, oproj=None, g_final=None):
    groups, rows, d = x.shape
    dff = w_down.shape[0]
    pad = -(-2 * stride // SUBLANES) * SUBLANES
    kern = functools.partial(_ffn_kernel, stride=stride, pad=pad, tm=tm, dff=dff,
                             oproj=oproj is not None, final=g_final is not None)
    row_spec = pl.BlockSpec((1, tm, d), lambda gi, t: (gi, t, 0))
    args, in_specs = [x], [row_spec]
    if oproj is not None:
        args += list(oproj)
        in_specs += [row_spec, _const_spec((d, d))]
    args += [prev, g, w_up, w_dw, b_dw, w_down]
    in_specs += [
        pl.BlockSpec((1, 2 * stride, dff), lambda gi, t: (gi, 0, 0)),
        _const_spec((1, d)),
        _const_spec((d, 2 * dff)),
        _const_spec((CONV_WIDTH, dff)),
        _const_spec((1, dff)),
        _const_spec((dff, d)),
    ]
    if g_final is not None:
        args.append(g_final)
        in_specs.append(_const_spec((1, d)))
    return pl.pallas_call(
        kern,
        grid=(groups, rows // tm),
        in_specs=in_specs,
        out_specs=[
            row_spec,
            pl.BlockSpec((1, 2 * stride, dff), lambda gi, t: (gi, 0, 0)),
        ],
        out_shape=[
            jax.ShapeDtypeStruct((groups, rows, d), F32),
            jax.ShapeDtypeStruct((groups, 2 * stride, dff), F32),
        ],
        scratch_shapes=[pltpu.VMEM((pad + tm, dff), F32)],
        compiler_params=pltpu.CompilerParams(
            dimension_semantics=("arbitrary", "arbitrary"), vmem_limit_bytes=VMEM_LIMIT),
        name="ffn",
    )(*args)


def _qkv_kernel(x_ref, g_ref, wq_ref, wkt_ref, wvt_ref,
                q_ref, kt_ref, vt_ref, ktb_ref, vtb_ref, *, scale):
    h = _rmsnorm(x_ref[0], g_ref[...]).astype(BF16)
    q_ref[0] = (_dot(h, wq_ref[...]) * scale).astype(BF16)
    n_blk, _, blk = ktb_ref.shape[1:]
    for w_ref, t_ref, tb_ref in ((wkt_ref, kt_ref, ktb_ref), (wvt_ref, vt_ref, vtb_ref)):
        t = lax.dot_general(w_ref[...], h, _NT, preferred_element_type=F32)
        t_ref[0] = t
        tb = t.astype(BF16)
        for c in range(n_blk):
            tb_ref[0, c] = tb[:, c * blk:(c + 1) * blk]


def _qkv(x, g, wq, wkt, wvt, *, tm, blk, scale):
    groups, rows, d = x.shape
    t_spec = pl.BlockSpec((1, d, tm), lambda gi, t: (gi, 0, t))
    tb_spec = pl.BlockSpec((1, tm // blk, d, blk), lambda gi, t: (gi, t, 0, 0))
    return pl.pallas_call(
        functools.partial(_qkv_kernel, scale=scale),
        grid=(groups, rows // tm),
        in_specs=[pl.BlockSpec((1, tm, d), lambda gi, t: (gi, t, 0)), _const_spec((1, d)),
                  _const_spec((d, d)), _const_spec((d, d)), _const_spec((d, d))],
        out_specs=[pl.BlockSpec((1, tm, d), lambda gi, t: (gi, t, 0)),
                   t_spec, t_spec, tb_spec, tb_spec],
        out_shape=[
            jax.ShapeDtypeStruct((groups, rows, d), BF16),
            jax.ShapeDtypeStruct((groups, d, rows), F32),
            jax.ShapeDtypeStruct((groups, d, rows), F32),
            jax.ShapeDtypeStruct((groups, rows // blk, d, blk), BF16),
            jax.ShapeDtypeStruct((groups, rows // blk, d, blk), BF16),
        ],
        compiler_params=pltpu.CompilerParams(
            dimension_semantics=("arbitrary", "arbitrary"), vmem_limit_bytes=VMEM_LIMIT),
        name="qkv",
    )(x, g, wq, wkt, wvt)


def _softplus2(z):
    return jnp.maximum(jnp.log2(1.0 + jnp.exp2(jnp.minimum(z, SOFTPLUS2_THRESHOLD))), z)


def _attn_prompt_kernel(bias_ref, q_ref, k_ref, v_ref, tri_ref, o_ref,
                        z_ref, suf_ref, acc_ref, *, tq, dh):
    n_heads = acc_ref.shape[0]
    n_all = bias_ref.shape[0] // BIAS_PIECES
    first = pl.program_id(1) * n_heads
    i = pl.program_id(2)
    lane = lax.broadcasted_iota(jnp.int32, (1, LANES), 1)
    first_head = lane < dh

    def pair(h):
        return slice((h // 2) * LANES, (h // 2 + 1) * LANES)

    q_heads = []
    for h in range(n_heads):
        q = q_ref[0, :, pair(h)]
        keep = first_head if h % 2 == 0 else jnp.logical_not(first_head)
        pieces = jnp.zeros((1, LANES), F32)
        for p in range(BIAS_PIECES):
            pieces = jnp.where(lane == p, bias_ref[p * n_all + first + h], pieces)
        extra = jnp.broadcast_to(pieces.astype(BF16), (tq, LANES))
        q_heads.append(jnp.concatenate([jnp.where(keep, q, jnp.zeros_like(q)), extra], axis=1))
    ones_rows = (lax.broadcasted_iota(jnp.int32, (LANES, tq), 0) < BIAS_PIECES).astype(BF16)
    acc_ref[...] = jnp.zeros_like(acc_ref)

    def block(item):
        return i - jnp.minimum(item, i)

    def scores(item, parity):
        for h in range(n_heads):
            if h % 2 == 0:
                kt = jnp.concatenate([k_ref[0, block(item), pair(h), :], ones_rows], axis=0)
            z_ref[parity, h] = _dot(q_heads[h], kt)

    def suffix(item, parity, later, mask=None):
        out = []
        for h in range(n_heads):
            z = z_ref[parity, h]
            if mask is not None:
                z = jnp.where(mask, z, MASKED)
                z_ref[parity, h] = z
            suf = _dot(_softplus2(z).astype(BF16), tri_ref[...])
            suf_ref[h] = suf + jnp.where(item > i, -MASKED, later[h])
            out.append(later[h] + suf[:, 0:1])
        return tuple(out)

    def weights(item, parity):
        for h in range(n_heads):
            w = jnp.exp2(z_ref[parity, h] - suf_ref[h]).astype(BF16)
            acc_ref[h] = acc_ref[h] + lax.dot_general(
                w, v_ref[0, block(item), pair(h), :], _NT, preferred_element_type=F32)

    t_idx = lax.broadcasted_iota(jnp.int32, (tq, tq), 0)
    s_idx = lax.broadcasted_iota(jnp.int32, (tq, tq), 1)
    none_later = jnp.zeros((tq, 1), F32)
    scores(0, 0)
    later = suffix(0, 0, (none_later,) * n_heads, s_idx < t_idx)
    scores(1, 1)

    def trip(m, later):
        n = 2 * m
        weights(n, 0)
        later = suffix(n + 1, 1, later)
        scores(n + 2, 0)
        weights(n + 1, 1)
        later = suffix(n + 2, 0, later)
        scores(n + 3, 1)
        return later

    lax.fori_loop(0, (i + 2) // 2, trip, later)
    for h in range(0, n_heads, 2):
        o_ref[0, :, pair(h)] = jnp.where(first_head, acc_ref[h], acc_ref[h + 1]).astype(BF16)


def _attn_prompt(q, ktb, vtb, bias, tri, *, heads_per_step):
    batch, seq, d = q.shape
    tq = ktb.shape[3]
    dh = d // N_HEADS
    width = heads_per_step * dh
    pieces, rest = [], bias
    for _ in range(BIAS_PIECES):
        pieces.append(rest.astype(BF16).astype(F32))
        rest = rest - pieces[-1]
    bias = jnp.concatenate(pieces)
    grid_spec = pltpu.PrefetchScalarGridSpec(
        num_scalar_prefetch=1,
        grid=(batch, d // width, seq // tq),
        in_specs=[
            pl.BlockSpec((1, tq, width), lambda b, hg, i, _: (b, i, hg)),
            pl.BlockSpec((1, seq // tq, width, tq), lambda b, hg, i, _: (b, 0, hg, 0)),
            pl.BlockSpec((1, seq // tq, width, tq), lambda b, hg, i, _: (b, 0, hg, 0)),
            pl.BlockSpec((tq, tq), lambda b, hg, i, _: (0, 0)),
        ],
        out_specs=pl.BlockSpec((1, tq, width), lambda b, hg, i, _: (b, i, hg)),
        scratch_shapes=[
            pltpu.VMEM((2, heads_per_step, tq, tq), F32),
            pltpu.VMEM((heads_per_step, tq, tq), F32),
            pltpu.VMEM((heads_per_step, tq, LANES), F32),
        ],
    )
    return pl.pallas_call(
        functools.partial(_attn_prompt_kernel, tq=tq, dh=dh),
        grid_spec=grid_spec,
        out_shape=jax.ShapeDtypeStruct((batch, seq, d), BF16),
        compiler_params=pltpu.CompilerParams(
            dimension_semantics=("arbitrary",) * 3, vmem_limit_bytes=VMEM_LIMIT),
        name="attn_prompt",
    )(bias, q, ktb, vtb, tri)


def _attn_sample_kernel(*refs, n_pp, page, dh):
    pt_ref, q_ref, kn_ref, vn_ref = refs[:4]
    kc_refs = refs[4:4 + n_pp]
    vc_refs = refs[4 + n_pp:4 + 2 * n_pp]
    bias_ref, tri_ref, o_ref, qbd_ref, acc_new_ref, acc_t_ref, later_ref = refs[4 + 2 * n_pp:]
    del pt_ref
    p = pl.program_id(1)
    rows, d = acc_new_ref.shape

    @pl.when(p == 0)
    def _():
        q_all = jnp.broadcast_to(q_ref[0][None], (N_HEADS, SUBLANES, d)).reshape(rows, d)
        row_head = lax.div(lax.broadcasted_iota(jnp.int32, (rows, d), 0), SUBLANES)
        col_head = lax.div(lax.broadcasted_iota(jnp.int32, (rows, d), 1), dh)
        qbd = jnp.where(row_head == col_head, q_all, 0.0).astype(BF16)
        qbd_ref[...] = qbd
        acc_t_ref[...] = jnp.zeros_like(acc_t_ref)
        fill = jnp.zeros((page - SUBLANES, d), F32)
        k_blk = jnp.concatenate([kn_ref[0], fill], axis=0).astype(BF16)
        v_blk = jnp.concatenate([vn_ref[0], fill], axis=0).astype(BF16)
        t_idx = lax.rem(lax.broadcasted_iota(jnp.int32, (rows, page), 0), SUBLANES)
        s_idx = lax.broadcasted_iota(jnp.int32, (rows, page), 1)
        z = lax.dot_general(qbd, k_blk, _NT, preferred_element_type=F32) + bias_ref[...]
        z = jnp.where(s_idx < t_idx, z, MASKED)
        suf = _dot(_softplus2(z).astype(BF16), tri_ref[0:page, 0:page])
        acc_new_ref[...] = _dot(jnp.exp2(z - suf).astype(BF16), v_blk)
        later_ref[...] = suf[:, 0:1]

    later = later_ref[...]
    contrib = None
    for b in range(n_pp // 2):
        kt = jnp.concatenate([kc_refs[2 * b + 1][0], kc_refs[2 * b][0]], axis=1).astype(BF16)
        vt = jnp.concatenate([vc_refs[2 * b + 1][0], vc_refs[2 * b][0]], axis=1).astype(BF16)
        z = _dot(qbd_ref[...], kt) + bias_ref[...]
        suf = _dot(_softplus2(z).astype(BF16), tri_ref[...])
        w = jnp.exp2(z - (suf + later)).astype(BF16)
        pv = lax.dot_general(vt, w, _NT, preferred_element_type=F32)
        contrib = pv if contrib is None else contrib + pv
        later = later + suf[:, 0:1]
    acc_t_ref[...] = acc_t_ref[...] + contrib
    later_ref[...] = later

    @pl.when(p == pl.num_programs(1) - 1)
    def _():
        acc = acc_new_ref[...] + acc_t_ref[...].T
        col_head = lax.div(lax.broadcasted_iota(jnp.int32, (SUBLANES, d), 1), dh)
        out = jnp.zeros((SUBLANES, d), F32)
        for h in range(N_HEADS):
            out = jnp.where(col_head == h, acc[h * SUBLANES:(h + 1) * SUBLANES, :], out)
        o_ref[0] = out


def _attn_sample(page_table, q, k_new, v_new, cache_kt, cache_vt, bias_rows, tri):
    n_seq, n_pages = page_table.shape
    _, d, page = cache_kt.shape
    dh = d // N_HEADS
    rows = N_HEADS * SUBLANES
    n_pp = PAGES_PER_STEP

    def cache_spec(slot):
        def index_map(s, p, pt):
            return (pt[s * n_pages + n_pages - 1 - (p * n_pp + slot)], 0, 0)
        return pl.BlockSpec((1, d, page), index_map)

    seq_spec = pl.BlockSpec((1, SUBLANES, d), lambda s, p, pt: (s, 0, 0))
    cache_specs = [cache_spec(slot) for slot in range(n_pp)]
    grid_spec = pltpu.PrefetchScalarGridSpec(
        num_scalar_prefetch=1,
        grid=(n_seq, n_pages // n_pp),
        in_specs=[seq_spec, seq_spec, seq_spec] + cache_specs + cache_specs + [
            pl.BlockSpec((rows, 1), lambda s, p, pt: (0, 0)),
            pl.BlockSpec((2 * page, 2 * page), lambda s, p, pt: (0, 0)),
        ],
        out_specs=seq_spec,
        scratch_shapes=[
            pltpu.VMEM((rows, d), BF16),
            pltpu.VMEM((rows, d), F32),
            pltpu.VMEM((d, rows), F32),
            pltpu.VMEM((rows, 1), F32),
        ],
    )
    return pl.pallas_call(
        functools.partial(_attn_sample_kernel, n_pp=n_pp, page=page, dh=dh),
        grid_spec=grid_spec,
        out_shape=jax.ShapeDtypeStruct((n_seq, SUBLANES, d), F32),
        compiler_params=pltpu.CompilerParams(
            dimension_semantics=("arbitrary", "arbitrary"), vmem_limit_bytes=VMEM_LIMIT),
        name="attn_sample",
    )(page_table.reshape(-1), q, k_new, v_new, *([cache_kt] * n_pp), *([cache_vt] * n_pp),
      bias_rows, tri)


def _suffix_ones(n):
    r = lax.broadcasted_iota(jnp.int32, (n, n), 0)
    c = lax.broadcasted_iota(jnp.int32, (n, n), 1)
    return (r >= c).astype(BF16)


def kernel(x_prompt, x_sample, cache_k, cache_v, state_conv, state_ffn, page_table, norm_mix, norm_ffn, norm_final, w_sc_in, w_sc_dw, w_sc_out, w_qkv, w_o, sb_bias, w_ffn_up, w_ffn_dw, b_ffn_dw, w_ffn_down):
    bp, tp, d = x_prompt.shape
    bs, ts, _ = x_sample.shape
    dff = w_ffn_down.shape[1]
    dh = d // N_HEADS
    page = cache_k.shape[2]
    hist = CONV_WIDTH - 1
    scale = LOG2_E / math.sqrt(dh)

    w_in, w_out = w_sc_in[0].astype(BF16), w_sc_out[0].astype(BF16)
    wq = w_qkv[0, :, 0:d].astype(BF16)
    wkt = w_qkv[0, :, d:2 * d].T.astype(BF16)
    wvt = w_qkv[0, :, 2 * d:3 * d].T.astype(BF16)
    wo = w_o[0].astype(BF16)
    w_up, w_down = w_ffn_up.astype(BF16), w_ffn_down.astype(BF16)
    g_mix = norm_mix[:, None, :]
    g_ffn = norm_ffn[:, None, :]
    g_fin = norm_final[None, :]
    b_dw = b_ffn_dw[:, None, :]
    bias = sb_bias[0] * LOG2_E

    def trunk(x, conv_prev, ffn_prev, attend, *, stride, tm_mix, tm_ffn, tm_qkv):
        groups, rows, _ = x.shape
        x, conv_new = _mixer0(x, conv_prev, g_mix[0], w_in, w_sc_dw[0], w_out,
                              stride=stride, tm=tm_mix)
        x, ffn_new0 = _ffn(x, ffn_prev[0], g_ffn[0], w_up[0], w_ffn_dw[0], b_dw[0], w_down[0],
                           stride=stride, tm=tm_ffn)
        q, kt, vt, ktb, vtb = _qkv(x, g_mix[1], wq, wkt, wvt, tm=tm_qkv, blk=KEY_BLOCK, scale=scale)
        o = attend(q, kt, vt, ktb, vtb)
        y, ffn_new1 = _ffn(x, ffn_prev[1], g_ffn[1], w_up[1], w_ffn_dw[1], b_dw[1], w_down[1],
                           stride=stride, tm=tm_ffn, oproj=(o, wo), g_final=g_fin)
        return y, conv_new, (ffn_new0, ffn_new1), kt, vt

    def attend_prompt(q, kt, vt, ktb, vtb):
        return _attn_prompt(q, ktb, vtb, bias, _suffix_ones(KEY_BLOCK), heads_per_step=4)

    y_p, conv_p, ffn_p, kt_p, vt_p = trunk(
        x_prompt, jnp.zeros((bp, hist, d), F32), jnp.zeros((2, bp, hist, dff), F32),
        attend_prompt, stride=1, tm_mix=512, tm_ffn=256, tm_qkv=512)

    def to_rows(a):
        return a.transpose(1, 0, 2).reshape(1, a.shape[1] * bs, a.shape[2])

    def from_rows(a, t):
        return a.reshape(t, bs, a.shape[-1]).transpose(1, 0, 2)

    def per_seq_tile(a):
        return jnp.pad(from_rows(a, ts).astype(F32), ((0, 0), (0, SUBLANES - ts), (0, 0)))

    def pages_t(cache):
        return cache.transpose(0, 2, 3, 1).reshape(cache.shape[0], d, page)

    def attend_sample(q, kt, vt, ktb, vtb):
        bias_rows = jnp.repeat(bias, SUBLANES)[:, None]
        o = _attn_sample(page_table, per_seq_tile(q[0]), per_seq_tile(kt[0].T), per_seq_tile(vt[0].T),
                         pages_t(cache_k[0]), pages_t(cache_v[0]),
                         bias_rows, _suffix_ones(2 * page))
        return to_rows(o[:, :ts]).astype(BF16)

    y_s, conv_s, ffn_s, kt_s, vt_s = trunk(
        to_rows(x_sample), to_rows(state_conv[0]), jnp.stack([to_rows(state_ffn[0]), to_rows(state_ffn[1])]),
        attend_sample, stride=bs, tm_mix=ts * bs, tm_ffn=ts * bs, tm_qkv=ts * bs)

    heads = (N_HEADS, dh)

    def prompt_heads(t):
        return t.reshape(bp, *heads, tp).transpose(0, 3, 1, 2)[None]

    return (
        y_p,
        from_rows(y_s[0], ts),
        prompt_heads(kt_p),
        prompt_heads(vt_p),
        conv_p[None],
        jnp.stack(ffn_p),
        from_rows(kt_s[0].T, ts).reshape(1, bs, ts, *heads),
        from_rows(vt_s[0].T, ts).reshape(1, bs, ts, *heads),
        from_rows(conv_s[0], hist)[None],
        jnp.stack([from_rows(f[0], hist) for f in ffn_s]),
    )
```

```python
import functools
import math

import jax
import jax.numpy as jnp
from jax import lax
from jax.experimental import pallas as pl
from jax.experimental.pallas import tpu as pltpu

F32 = jnp.float32
BF16 = jnp.bfloat16

NORM_EPS = 1e-6
CONV_WIDTH = 3
N_HEADS = 16
SUBLANES = 8
LANES = 128
VMEM_LIMIT = 56 * 1024 * 1024
LOG2_E = math.log2(math.e)
MASKED = -1e30
SOFTPLUS2_THRESHOLD = 64.0
BIAS_PIECES = 3
KEY_BLOCK = 256

_NT = (((1,), (1,)), ((), ()))


def _rmsnorm(x, g):
    ms = jnp.mean(x * x, axis=-1, keepdims=True)
    return x * lax.rsqrt(ms + NORM_EPS) * g


def _dot(a, b):
    return jnp.dot(a, b, preferred_element_type=F32)


def _const_spec(shape):
    return pl.BlockSpec(shape, lambda *_: (0,) * len(shape), pipeline_mode=pl.Buffered(1))


def _conv_rows(ext_ref, cur, w_ref, *, stride, pad, tm):
    ext_ref[pad:pad + tm, :] = cur
    lo = pad - 2 * stride
    y = ext_ref[lo:lo + tm, :] * w_ref[0:1, :]
    y = y + ext_ref[lo + stride:lo + stride + tm, :] * w_ref[1:2, :]
    return y + cur * w_ref[2:3, :]


def _roll_history(ext_ref, st_ref, *, stride, pad, tm):
    hist = ext_ref[pad + tm - 2 * stride:pad + tm, :]
    st_ref[0] = hist
    ext_ref[pad - 2 * stride:pad, :] = hist


def _mixer0_kernel(x_ref, prev_ref, g_ref, win_ref, wdw_ref, wout_ref,
                   y_ref, st_ref, ext_ref, *, stride, pad, tm, d):
    @pl.when(pl.program_id(1) == 0)
    def _():
        ext_ref[pad - 2 * stride:pad, :] = prev_ref[0]

    x = x_ref[0]
    h = _rmsnorm(x, g_ref[...]).astype(BF16)
    b_gate = _dot(h, win_ref[:, 0:d])
    u = _dot(h, win_ref[:, d:2 * d]) * _dot(h, win_ref[:, 2 * d:3 * d])
    uc = _conv_rows(ext_ref, u, wdw_ref, stride=stride, pad=pad, tm=tm)
    y_ref[0] = x + _dot((b_gate * uc).astype(BF16), wout_ref[...])
    _roll_history(ext_ref, st_ref, stride=stride, pad=pad, tm=tm)


def _mixer0(x, prev, g, w_in, w_dw, w_out, *, stride, tm):
    groups, rows, d = x.shape
    pad = -(-2 * stride // SUBLANES) * SUBLANES
    kern = functools.partial(_mixer0_kernel, stride=stride, pad=pad, tm=tm, d=d)
    return pl.pallas_call(
        kern,
        grid=(groups, rows // tm),
        in_specs=[
            pl.BlockSpec((1, tm, d), lambda gi, t: (gi, t, 0)),
            pl.BlockSpec((1, 2 * stride, d), lambda gi, t: (gi, 0, 0)),
            _const_spec((1, d)),
            _const_spec((d, 3 * d)),
            _const_spec((CONV_WIDTH, d)),
            _const_spec((d, d)),
        ],
        out_specs=[
            pl.BlockSpec((1, tm, d), lambda gi, t: (gi, t, 0)),
            pl.BlockSpec((1, 2 * stride, d), lambda gi, t: (gi, 0, 0)),
        ],
        out_shape=[
            jax.ShapeDtypeStruct((groups, rows, d), F32),
            jax.ShapeDtypeStruct((groups, 2 * stride, d), F32),
        ],
        scratch_shapes=[pltpu.VMEM((pad + tm, d), F32)],
        compiler_params=pltpu.CompilerParams(
            dimension_semantics=("arbitrary", "arbitrary"), vmem_limit_bytes=VMEM_LIMIT),
        name="mixer0",
    )(x, prev, g, w_in, w_dw, w_out)


def _ffn_kernel(*refs, stride, pad, tm, dff, oproj, final):
    refs = list(refs)
    x_ref = refs.pop(0)
    if oproj:
        o_ref, wo_ref = refs.pop(0), refs.pop(0)
    prev_ref, g_ref, wup_ref, wdw_ref, bdw_ref, wdown_ref = refs[:6]
    refs = refs[6:]
    if final:
        gf_ref = refs.pop(0)
    y_ref, st_ref, ext_ref = refs

    @pl.when(pl.program_id(1) == 0)
    def _():
        ext_ref[pad - 2 * stride:pad, :] = prev_ref[0]

    x = x_ref[0]
    if oproj:
        x = x + _dot(o_ref[0], wo_ref[...])
    h = _rmsnorm(x, g_ref[...]).astype(BF16)
    gate = _dot(h, wup_ref[:, 0:dff])
    up = _dot(h, wup_ref[:, dff:2 * dff])
    gc = _conv_rows(ext_ref, gate, wdw_ref, stride=stride, pad=pad, tm=tm) + bdw_ref[...]
    act = gc / (1.0 + jnp.exp(-gc)) * up
    xo = x + _dot(act.astype(BF16), wdown_ref[...])
    y_ref[0] = _rmsnorm(xo, gf_ref[...]) if final else xo
    _roll_history(ext_ref, st_ref, stride=stride, pad=pad, tm=tm)


def _ffn(x, prev, g, w_up, w_dw, b_dw, w_down, *, stride, tm, oproj=None, g_final=None):
    groups, rows, d = x.shape
    dff = w_down.shape[0]
    pad = -(-2 * stride // SUBLANES) * SUBLANES
    kern = functools.partial(_ffn_kernel, stride=stride, pad=pad, tm=tm, dff=dff,
                             oproj=oproj is not None, final=g_final is not None)
    row_spec = pl.BlockSpec((1, tm, d), lambda gi, t: (gi, t, 0))
    args, in_specs = [x], [row_spec]
    if oproj is not None:
        args += list(oproj)
        in_specs += [row_spec, _const_spec((d, d))]
    args += [prev, g, w_up, w_dw, b_dw, w_down]
    in_specs += [
        pl.BlockSpec((1, 2 * stride, dff), lambda gi, t: (gi, 0, 0)),
        _const_spec((1, d)),
        _const_spec((d, 2 * dff)),
        _const_spec((CONV_WIDTH, dff)),
        _const_spec((1, dff)),
        _const_spec((dff, d)),
    ]
    if g_final is not None:
        args.append(g_final)
        in_specs.append(_const_spec((1, d)))
    return pl.pallas_call(
        kern,
        grid=(groups, rows // tm),
        in_specs=in_specs,
        out_specs=[
            row_spec,
            pl.BlockSpec((1, 2 * stride, dff), lambda gi, t: (gi, 0, 0)),
        ],
        out_shape=[
            jax.ShapeDtypeStruct((groups, rows, d), F32),
            jax.ShapeDtypeStruct((groups, 2 * stride, dff), F32),
        ],
        scratch_shapes=[pltpu.VMEM((pad + tm, dff), F32)],
        compiler_params=pltpu.CompilerParams(
            dimension_semantics=("arbitrary", "arbitrary"), vmem_limit_bytes=VMEM_LIMIT),
        name="ffn",
    )(*args)


def _qkv_kernel(x_ref, g_ref, wq_ref, wkt_ref, wvt_ref,
                q_ref, kt_ref, vt_ref, ktb_ref, vtb_ref, *, scale):
    h = _rmsnorm(x_ref[0], g_ref[...]).astype(BF16)
    q_ref[0] = (_dot(h, wq_ref[...]) * scale).astype(BF16)
    n_blk, _, blk = ktb_ref.shape[1:]
    for w_ref, t_ref, tb_ref in ((wkt_ref, kt_ref, ktb_ref), (wvt_ref, vt_ref, vtb_ref)):
        t = lax.dot_general(w_ref[...], h, _NT, preferred_element_type=F32)
        t_ref[0] = t
        tb = t.astype(BF16)
        for c in range(n_blk):
            tb_ref[0, c] = tb[:, c * blk:(c + 1) * blk]


def _qkv(x, g, wq, wkt, wvt, *, tm, blk, scale):
    groups, rows, d = x.shape
    t_spec = pl.BlockSpec((1, d, tm), lambda gi, t: (gi, 0, t))
    tb_spec = pl.BlockSpec((1, tm // blk, d, blk), lambda gi, t: (gi, t, 0, 0))
    return pl.pallas_call(
        functools.partial(_qkv_kernel, scale=scale),
        grid=(groups, rows // tm),
        in_specs=[pl.BlockSpec((1, tm, d), lambda gi, t: (gi, t, 0)), _const_spec((1, d)),
                  _const_spec((d, d)), _const_spec((d, d)), _const_spec((d, d))],
        out_specs=[pl.BlockSpec((1, tm, d), lambda gi, t: (gi, t, 0)),
                   t_spec, t_spec, tb_spec, tb_spec],
        out_shape=[
            jax.ShapeDtypeStruct((groups, rows, d), BF16),
            jax.ShapeDtypeStruct((groups, d, rows), F32),
            jax.ShapeDtypeStruct((groups, d, rows), F32),
            jax.ShapeDtypeStruct((groups, rows // blk, d, blk), BF16),
            jax.ShapeDtypeStruct((groups, rows // blk, d, blk), BF16),
        ],
        compiler_params=pltpu.CompilerParams(
            dimension_semantics=("arbitrary", "arbitrary"), vmem_limit_bytes=VMEM_LIMIT),
        name="qkv",
    )(x, g, wq, wkt, wvt)


def _softplus2(z):
    return jnp.maximum(jnp.log2(1.0 + jnp.exp2(jnp.minimum(z, SOFTPLUS2_THRESHOLD))), z)


def _attn_prompt_kernel(bias_ref, q_ref, k_ref, v_ref, tri_ref, o_ref,
                        z_ref, acc_ref, *, tq, dh):
    n_heads = acc_ref.shape[0]
    n_all = bias_ref.shape[0] // BIAS_PIECES
    first = pl.program_id(1) * n_heads
    i = pl.program_id(2)
    lane = lax.broadcasted_iota(jnp.int32, (1, LANES), 1)
    first_head = lane < dh

    def pair(h):
        return slice((h // 2) * LANES, (h // 2 + 1) * LANES)

    q_heads = []
    for h in range(n_heads):
        q = q_ref[0, :, pair(h)]
        keep = first_head if h % 2 == 0 else jnp.logical_not(first_head)
        pieces = jnp.zeros((1, LANES), F32)
        for p in range(BIAS_PIECES):
            pieces = jnp.where(lane == p, bias_ref[p * n_all + first + h], pieces)
        extra = jnp.broadcast_to(pieces.astype(BF16), (tq, LANES))
        q_heads.append(jnp.concatenate([jnp.where(keep, q, jnp.zeros_like(q)), extra], axis=1))
    ones_rows = (lax.broadcasted_iota(jnp.int32, (LANES, tq), 0) < BIAS_PIECES).astype(BF16)
    acc_ref[...] = jnp.zeros_like(acc_ref)

    def block(item):
        return i - jnp.minimum(item, i)

    def scores(item, parity):
        for h in range(n_heads):
            if h % 2 == 0:
                kt = jnp.concatenate([k_ref[0, block(item), pair(h), :], ones_rows], axis=0)
            z_ref[parity, h] = _dot(q_heads[h], kt)

    def weights(item, parity, later, mask=None):
        out = []
        for h in range(n_heads):
            z = z_ref[parity, h]
            if mask is not None:
                z = jnp.where(mask, z, MASKED)
            suf = _dot(_softplus2(z).astype(BF16), tri_ref[...])
            total = suf + jnp.where(item > i, -MASKED, later[h])
            w = jnp.exp2(z - total).astype(BF16)
            acc_ref[h] = acc_ref[h] + lax.dot_general(
                w, v_ref[0, block(item), pair(h), :], _NT, preferred_element_type=F32)
            out.append(later[h] + suf[:, 0:1])
        return tuple(out)

    t_idx = lax.broadcasted_iota(jnp.int32, (tq, tq), 0)
    s_idx = lax.broadcasted_iota(jnp.int32, (tq, tq), 1)
    none_later = jnp.zeros((tq, 1), F32)
    scores(0, 0)
    scores(1, 1)
    later = weights(0, 0, (none_later,) * n_heads, s_idx < t_idx)
    scores(2, 0)

    def trip(m, later):
        n = 2 * m + 1
        later = weights(n, 1, later)
        scores(n + 2, 1)
        later = weights(n + 1, 0, later)
        scores(n + 3, 0)
        return later

    lax.fori_loop(0, (i + 1) // 2, trip, later)
    for h in range(0, n_heads, 2):
        o_ref[0, :, pair(h)] = jnp.where(first_head, acc_ref[h], acc_ref[h + 1]).astype(BF16)


def _attn_prompt(q, ktb, vtb, bias, tri, *, heads_per_step):
    batch, seq, d = q.shape
    tq = ktb.shape[3]
    dh = d // N_HEADS
    width = heads_per_step * dh
    pieces, rest = [], bias
    for _ in range(BIAS_PIECES):
        pieces.append(rest.astype(BF16).astype(F32))
        rest = rest - pieces[-1]
    bias = jnp.concatenate(pieces)
    grid_spec = pltpu.PrefetchScalarGridSpec(
        num_scalar_prefetch=1,
        grid=(batch, d // width, seq // tq),
        in_specs=[
            pl.BlockSpec((1, tq, width), lambda b, hg, i, _: (b, i, hg)),
            pl.BlockSpec((1, seq // tq, width, tq), lambda b, hg, i, _: (b, 0, hg, 0)),
            pl.BlockSpec((1, seq // tq, width, tq), lambda b, hg, i, _: (b, 0, hg, 0)),
            pl.BlockSpec((tq, tq), lambda b, hg, i, _: (0, 0)),
        ],
        out_specs=pl.BlockSpec((1, tq, width), lambda b, hg, i, _: (b, i, hg)),
        scratch_shapes=[
            pltpu.VMEM((2, heads_per_step, tq, tq), F32),
            pltpu.VMEM((heads_per_step, tq, LANES), F32),
        ],
    )
    return pl.pallas_call(
        functools.partial(_attn_prompt_kernel, tq=tq, dh=dh),
        grid_spec=grid_spec,
        out_shape=jax.ShapeDtypeStruct((batch, seq, d), BF16),
        compiler_params=pltpu.CompilerParams(
            dimension_semantics=("arbitrary",) * 3, vmem_limit_bytes=VMEM_LIMIT),
        name="attn_prompt",
    )(bias, q, ktb, vtb, tri)


def _attn_sample_kernel(*refs, n_pages, page, dh):
    pt_ref, q_ref, kn_ref, vn_ref = refs[:4]
    kc_refs = refs[4:4 + n_pages]
    vc_refs = refs[4 + n_pages:4 + 2 * n_pages]
    bias_ref, tri_ref, o_ref = refs[4 + 2 * n_pages:]
    del pt_ref
    d = q_ref.shape[2]
    rows = N_HEADS * SUBLANES

    q_all = jnp.broadcast_to(q_ref[0][None], (N_HEADS, SUBLANES, d)).reshape(rows, d)
    row_head = lax.div(lax.broadcasted_iota(jnp.int32, (rows, d), 0), SUBLANES)
    col_head = lax.div(lax.broadcasted_iota(jnp.int32, (rows, d), 1), dh)
    qbd = jnp.where(row_head == col_head, q_all, 0.0).astype(BF16)

    fill = jnp.zeros((page - SUBLANES, d), F32)
    k_blk = jnp.concatenate([kn_ref[0], fill], axis=0).astype(BF16)
    v_blk = jnp.concatenate([vn_ref[0], fill], axis=0).astype(BF16)
    t_idx = lax.rem(lax.broadcasted_iota(jnp.int32, (rows, page), 0), SUBLANES)
    s_idx = lax.broadcasted_iota(jnp.int32, (rows, page), 1)
    z = lax.dot_general(qbd, k_blk, _NT, preferred_element_type=F32) + bias_ref[...]
    z = jnp.where(s_idx < t_idx, z, MASKED)
    suf = _dot(_softplus2(z).astype(BF16), tri_ref[0:page, 0:page])
    acc_new = _dot(jnp.exp2(z - suf).astype(BF16), v_blk)
    later = suf[:, 0:1]

    acc_t = None
    for b in range(n_pages // 2):
        kt = jnp.concatenate([kc_refs[2 * b + 1][0], kc_refs[2 * b][0]], axis=1).astype(BF16)
        vt = jnp.concatenate([vc_refs[2 * b + 1][0], vc_refs[2 * b][0]], axis=1).astype(BF16)
        z = _dot(qbd, kt) + bias_ref[...]
        suf = _dot(_softplus2(z).astype(BF16), tri_ref[...])
        w = jnp.exp2(z - (suf + later)).astype(BF16)
        pv = lax.dot_general(vt, w, _NT, preferred_element_type=F32)
        acc_t = pv if acc_t is None else acc_t + pv
        later = later + suf[:, 0:1]

    acc = acc_new + acc_t.T
    out_head = lax.div(lax.broadcasted_iota(jnp.int32, (SUBLANES, d), 1), dh)
    out = jnp.zeros((SUBLANES, d), F32)
    for h in range(N_HEADS):
        out = jnp.where(out_head == h, acc[h * SUBLANES:(h + 1) * SUBLANES, :], out)
    o_ref[0] = out


def _attn_sample(page_table, q, k_new, v_new, cache_kt, cache_vt, bias_rows, tri):
    n_seq, n_pages = page_table.shape
    _, d, page = cache_kt.shape
    assert n_pages % 2 == 0
    rows = N_HEADS * SUBLANES

    def cache_spec(slot):
        return pl.BlockSpec((1, d, page),
                            lambda s, pt: (pt[s * n_pages + n_pages - 1 - slot], 0, 0))

    seq_spec = pl.BlockSpec((1, SUBLANES, d), lambda s, pt: (s, 0, 0))
    cache_specs = [cache_spec(slot) for slot in range(n_pages)]
    grid_spec = pltpu.PrefetchScalarGridSpec(
        num_scalar_prefetch=1,
        grid=(n_seq,),
        in_specs=[seq_spec, seq_spec, seq_spec] + cache_specs + cache_specs + [
            pl.BlockSpec((rows, 1), lambda s, pt: (0, 0)),
            pl.BlockSpec((2 * page, 2 * page), lambda s, pt: (0, 0)),
        ],
        out_specs=seq_spec,
    )
    return pl.pallas_call(
        functools.partial(_attn_sample_kernel, n_pages=n_pages, page=page, dh=d // N_HEADS),
        grid_spec=grid_spec,
        out_shape=jax.ShapeDtypeStruct((n_seq, SUBLANES, d), F32),
        compiler_params=pltpu.CompilerParams(
            dimension_semantics=("arbitrary",), vmem_limit_bytes=VMEM_LIMIT),
        name="attn_sample",
    )(page_table.reshape(-1), q, k_new, v_new, *([cache_kt] * n_pages), *([cache_vt] * n_pages),
      bias_rows, tri)


def _suffix_ones(n):
    r = lax.broadcasted_iota(jnp.int32, (n, n), 0)
    c = lax.broadcasted_iota(jnp.int32, (n, n), 1)
    return (r >= c).astype(BF16)


def kernel(x_prompt, x_sample, cache_k, cache_v, state_conv, state_ffn, page_table, norm_mix, norm_ffn, norm_final, w_sc_in, w_sc_dw, w_sc_out, w_qkv, w_o, sb_bias, w_ffn_up, w_ffn_dw, b_ffn_dw, w_ffn_down):
    bp, tp, d = x_prompt.shape
    bs, ts, _ = x_sample.shape
    dff = w_ffn_down.shape[1]
    dh = d // N_HEADS
    page = cache_k.shape[2]
    hist = CONV_WIDTH - 1
    scale = LOG2_E / math.sqrt(dh)

    w_in, w_out = w_sc_in[0].astype(BF16), w_sc_out[0].astype(BF16)
    wq = w_qkv[0, :, 0:d].astype(BF16)
    wkt = w_qkv[0, :, d:2 * d].T.astype(BF16)
    wvt = w_qkv[0, :, 2 * d:3 * d].T.astype(BF16)
    wo = w_o[0].astype(BF16)
    w_up, w_down = w_ffn_up.astype(BF16), w_ffn_down.astype(BF16)
    g_mix = norm_mix[:, None, :]
    g_ffn = norm_ffn[:, None, :]
    g_fin = norm_final[None, :]
    b_dw = b_ffn_dw[:, None, :]
    bias = sb_bias[0] * LOG2_E

    def trunk(x, conv_prev, ffn_prev, attend, *, stride, tm_mix, tm_ffn, tm_qkv):
        groups, rows, _ = x.shape
        x, conv_new = _mixer0(x, conv_prev, g_mix[0], w_in, w_sc_dw[0], w_out,
                              stride=stride, tm=tm_mix)
        x, ffn_new0 = _ffn(x, ffn_prev[0], g_ffn[0], w_up[0], w_ffn_dw[0], b_dw[0], w_down[0],
                           stride=stride, tm=tm_ffn)
        q, kt, vt, ktb, vtb = _qkv(x, g_mix[1], wq, wkt, wvt, tm=tm_qkv, blk=KEY_BLOCK, scale=scale)
        o = attend(q, kt, vt, ktb, vtb)
        y, ffn_new1 = _ffn(x, ffn_prev[1], g_ffn[1], w_up[1], w_ffn_dw[1], b_dw[1], w_down[1],
                           stride=stride, tm=tm_ffn, oproj=(o, wo), g_final=g_fin)
        return y, conv_new, (ffn_new0, ffn_new1), kt, vt

    def attend_prompt(q, kt, vt, ktb, vtb):
        return _attn_prompt(q, ktb, vtb, bias, _suffix_ones(KEY_BLOCK), heads_per_step=8)

    y_p, conv_p, ffn_p, kt_p, vt_p = trunk(
        x_prompt, jnp.zeros((bp, hist, d), F32), jnp.zeros((2, bp, hist, dff), F32),
        attend_prompt, stride=1, tm_mix=512, tm_ffn=256, tm_qkv=512)

    def to_rows(a):
        return a.transpose(1, 0, 2).reshape(1, a.shape[1] * bs, a.shape[2])

    def from_rows(a, t):
        return a.reshape(t, bs, a.shape[-1]).transpose(1, 0, 2)

    def per_seq_tile(a):
        return jnp.pad(from_rows(a, ts).astype(F32), ((0, 0), (0, SUBLANES - ts), (0, 0)))

    def pages_t(cache):
        return cache.transpose(0, 2, 3, 1).reshape(cache.shape[0], d, page)

    def attend_sample(q, kt, vt, ktb, vtb):
        bias_rows = jnp.repeat(bias, SUBLANES)[:, None]
        o = _attn_sample(page_table, per_seq_tile(q[0]), per_seq_tile(kt[0].T), per_seq_tile(vt[0].T),
                         pages_t(cache_k[0]), pages_t(cache_v[0]),
                         bias_rows, _suffix_ones(2 * page))
        return to_rows(o[:, :ts]).astype(BF16)

    y_s, conv_s, ffn_s, kt_s, vt_s = trunk(
        to_rows(x_sample), to_rows(state_conv[0]), jnp.stack([to_rows(state_ffn[0]), to_rows(state_ffn[1])]),
        attend_sample, stride=bs, tm_mix=ts * bs, tm_ffn=ts * bs, tm_qkv=ts * bs)

    heads = (N_HEADS, dh)

    def prompt_heads(t):
        return t.reshape(bp, *heads, tp).transpose(0, 3, 1, 2)[None]

    return (
        y_p,
        from_rows(y_s[0], ts),
        prompt_heads(kt_p),
        prompt_heads(vt_p),
        conv_p[None],
        jnp.stack(ffn_p),
        from_rows(kt_s[0].T, ts).reshape(1, bs, ts, *heads),
        from_rows(vt_s[0].T, ts).reshape(1, bs, ts, *heads),
        from_rows(conv_s[0], hist)[None],
        jnp.stack([from_rows(f[0], hist) for f in ffn_s]),
    )
```

```python
import functools
import math

import jax
import jax.numpy as jnp
from jax import lax
from jax.experimental import pallas as pl
from jax.experimental.pallas import tpu as pltpu

F32 = jnp.float32
BF16 = jnp.bfloat16

NORM_EPS = 1e-6
CONV_WIDTH = 3
N_HEADS = 16
SUBLANES = 8
LANES = 128
VMEM_LIMIT = 56 * 1024 * 1024
LOG2_E = math.log2(math.e)
MASKED = -1e30
SOFTPLUS2_THRESHOLD = 64.0
BIAS_PIECES = 3
KEY_BLOCK = 256

_NT = (((1,), (1,)), ((), ()))


def _rmsnorm(x, g):
    ms = jnp.mean(x * x, axis=-1, keepdims=True)
    return x * lax.rsqrt(ms + NORM_EPS) * g


def _dot(a, b):
    return jnp.dot(a, b, preferred_element_type=F32)


def _const_spec(shape):
    return pl.BlockSpec(shape, lambda *_: (0,) * len(shape), pipeline_mode=pl.Buffered(1))


def _conv_rows(ext_ref, cur, w_ref, *, stride, pad, tm):
    ext_ref[pad:pad + tm, :] = cur
    lo = pad - 2 * stride
    y = ext_ref[lo:lo + tm, :] * w_ref[0:1, :]
    y = y + ext_ref[lo + stride:lo + stride + tm, :] * w_ref[1:2, :]
    return y + cur * w_ref[2:3, :]


def _roll_history(ext_ref, st_ref, *, stride, pad, tm):
    hist = ext_ref[pad + tm - 2 * stride:pad + tm, :]
    st_ref[0] = hist
    ext_ref[pad - 2 * stride:pad, :] = hist


def _mixer0_kernel(x_ref, prev_ref, g_ref, win_ref, wdw_ref, wout_ref,
                   y_ref, st_ref, ext_ref, *, stride, pad, tm, d):
    @pl.when(pl.program_id(1) == 0)
    def _():
        ext_ref[pad - 2 * stride:pad, :] = prev_ref[0]

    x = x_ref[0]
    h = _rmsnorm(x, g_ref[...]).astype(BF16)
    b_gate = _dot(h, win_ref[:, 0:d])
    u = _dot(h, win_ref[:, d:2 * d]) * _dot(h, win_ref[:, 2 * d:3 * d])
    uc = _conv_rows(ext_ref, u, wdw_ref, stride=stride, pad=pad, tm=tm)
    y_ref[0] = x + _dot((b_gate * uc).astype(BF16), wout_ref[...])
    _roll_history(ext_ref, st_ref, stride=stride, pad=pad, tm=tm)


def _mixer0(x, prev, g, w_in, w_dw, w_out, *, stride, tm):
    groups, rows, d = x.shape
    pad = -(-2 * stride // SUBLANES) * SUBLANES
    kern = functools.partial(_mixer0_kernel, stride=stride, pad=pad, tm=tm, d=d)
    return pl.pallas_call(
        kern,
        grid=(groups, rows // tm),
        in_specs=[
            pl.BlockSpec((1, tm, d), lambda gi, t: (gi, t, 0)),
            pl.BlockSpec((1, 2 * stride, d), lambda gi, t: (gi, 0, 0)),
            _const_spec((1, d)),
            _const_spec((d, 3 * d)),
            _const_spec((CONV_WIDTH, d)),
            _const_spec((d, d)),
        ],
        out_specs=[
            pl.BlockSpec((1, tm, d), lambda gi, t: (gi, t, 0)),
            pl.BlockSpec((1, 2 * stride, d), lambda gi, t: (gi, 0, 0)),
        ],
        out_shape=[
            jax.ShapeDtypeStruct((groups, rows, d), F32),
            jax.ShapeDtypeStruct((groups, 2 * stride, d), F32),
        ],
        scratch_shapes=[pltpu.VMEM((pad + tm, d), F32)],
        compiler_params=pltpu.CompilerParams(
            dimension_semantics=("arbitrary", "arbitrary"), vmem_limit_bytes=VMEM_LIMIT),
        name="mixer0",
    )(x, prev, g, w_in, w_dw, w_out)


def _ffn_kernel(*refs, stride, pad, tm, dff, oproj, final):
    refs = list(refs)
    x_ref = refs.pop(0)
    if oproj:
        o_ref, wo_ref = refs.pop(0), refs.pop(0)
    prev_ref, g_ref, wup_ref, wdw_ref, bdw_ref, wdown_ref = refs[:6]
    refs = refs[6:]
    if final:
        gf_ref = refs.pop(0)
    y_ref, st_ref, ext_ref = refs

    @pl.when(pl.program_id(1) == 0)
    def _():
        ext_ref[pad - 2 * stride:pad, :] = prev_ref[0]

    x = x_ref[0]
    if oproj:
        x = x + _dot(o_ref[0], wo_ref[...])
    h = _rmsnorm(x, g_ref[...]).astype(BF16)
    gate = _dot(h, wup_ref[:, 0:dff])
    up = _dot(h, wup_ref[:, dff:2 * dff])
    gc = _conv_rows(ext_ref, gate, wdw_ref, stride=stride, pad=pad, tm=tm) + bdw_ref[...]
    act = gc / (1.0 + jnp.exp(-gc)) * up
    xo = x + _dot(act.astype(BF16), wdown_ref[...])
    y_ref[0] = _rmsnorm(xo, gf_ref[...]) if final else xo
    _roll_history(ext_ref, st_ref, stride=stride, pad=pad, tm=tm)


def _ffn(x, prev, g, w_up, w_dw, b_dw, w_down, *, stride, tm, oproj=None, g_final=None):
    groups, rows, d = x.shape
    dff = w_down.shape[0]
    pad = -(-2 * stride // SUBLANES) * SUBLANES
    kern = functools.partial(_ffn_kernel, stride=stride, pad=pad, tm=tm, dff=dff,
                             oproj=oproj is not None, final=g_final is not None)
    row_spec = pl.BlockSpec((1, tm, d), lambda gi, t: (gi, t, 0))
    args, in_specs = [x], [row_spec]
    if oproj is not None:
        args += list(oproj)
        in_specs += [row_spec, _const_spec((d, d))]
    args += [prev, g, w_up, w_dw, b_dw, w_down]
    in_specs += [
        pl.BlockSpec((1, 2 * stride, dff), lambda gi, t: (gi, 0, 0)),
        _const_spec((1, d)),
        _const_spec((d, 2 * dff)),
        _const_spec((CONV_WIDTH, dff)),
        _const_spec((1, dff)),
        _const_spec((dff, d)),
    ]
    if g_final is not None:
        args.append(g_final)
        in_specs.append(_const_spec((1, d)))
    return pl.pallas_call(
        kern,
        grid=(groups, rows // tm),
        in_specs=in_specs,
        out_specs=[
            row_spec,
            pl.BlockSpec((1, 2 * stride, dff), lambda gi, t: (gi, 0, 0)),
        ],
        out_shape=[
            jax.ShapeDtypeStruct((groups, rows, d), F32),
            jax.ShapeDtypeStruct((groups, 2 * stride, dff), F32),
        ],
        scratch_shapes=[pltpu.VMEM((pad + tm, dff), F32)],
        compiler_params=pltpu.CompilerParams(
            dimension_semantics=("arbitrary", "arbitrary"), vmem_limit_bytes=VMEM_LIMIT),
        name="ffn",
    )(*args)


def _qkv_kernel(x_ref, g_ref, wq_ref, wkt_ref, wvt_ref,
                q_ref, kt_ref, vt_ref, ktb_ref, vtb_ref, *, scale):
    h = _rmsnorm(x_ref[0], g_ref[...]).astype(BF16)
    q_ref[0] = (_dot(h, wq_ref[...]) * scale).astype(BF16)
    n_blk, _, blk = ktb_ref.shape[1:]
    for w_ref, t_ref, tb_ref in ((wkt_ref, kt_ref, ktb_ref), (wvt_ref, vt_ref, vtb_ref)):
        t = lax.dot_general(w_ref[...], h, _NT, preferred_element_type=F32)
        t_ref[0] = t
        tb = t.astype(BF16)
        for c in range(n_blk):
            tb_ref[0, c] = tb[:, c * blk:(c + 1) * blk]


def _qkv(x, g, wq, wkt, wvt, *, tm, blk, scale):
    groups, rows, d = x.shape
    t_spec = pl.BlockSpec((1, d, tm), lambda gi, t: (gi, 0, t))
    tb_spec = pl.BlockSpec((1, tm // blk, d, blk), lambda gi, t: (gi, t, 0, 0))
    return pl.pallas_call(
        functools.partial(_qkv_kernel, scale=scale),
        grid=(groups, rows // tm),
        in_specs=[pl.BlockSpec((1, tm, d), lambda gi, t: (gi, t, 0)), _const_spec((1, d)),
                  _const_spec((d, d)), _const_spec((d, d)), _const_spec((d, d))],
        out_specs=[pl.BlockSpec((1, tm, d), lambda gi, t: (gi, t, 0)),
                   t_spec, t_spec, tb_spec, tb_spec],
        out_shape=[
            jax.ShapeDtypeStruct((groups, rows, d), BF16),
            jax.ShapeDtypeStruct((groups, d, rows), F32),
            jax.ShapeDtypeStruct((groups, d, rows), F32),
            jax.ShapeDtypeStruct((groups, rows // blk, d, blk), BF16),
            jax.ShapeDtypeStruct((groups, rows // blk, d, blk), BF16),
        ],
        compiler_params=pltpu.CompilerParams(
            dimension_semantics=("arbitrary", "arbitrary"), vmem_limit_bytes=VMEM_LIMIT),
        name="qkv",
    )(x, g, wq, wkt, wvt)


def _softplus2(z):
    return jnp.maximum(jnp.log2(1.0 + jnp.exp2(jnp.minimum(z, SOFTPLUS2_THRESHOLD))), z)


def _attn_prompt_kernel(bias_ref, q_ref, k_ref, v_ref, tri_ref, o_ref,
                        z_ref, acc_ref, *, tq, dh):
    n_heads = acc_ref.shape[0]
    n_all = bias_ref.shape[0] // BIAS_PIECES
    first = pl.program_id(1) * n_heads
    i = pl.program_id(2)
    lane = lax.broadcasted_iota(jnp.int32, (1, LANES), 1)
    first_head = lane < dh

    def pair(h):
        return slice((h // 2) * LANES, (h // 2 + 1) * LANES)

    q_heads = []
    for h in range(n_heads):
        q = q_ref[0, :, pair(h)]
        keep = first_head if h % 2 == 0 else jnp.logical_not(first_head)
        pieces = jnp.zeros((1, LANES), F32)
        for p in range(BIAS_PIECES):
            pieces = jnp.where(lane == p, bias_ref[p * n_all + first + h], pieces)
        extra = jnp.broadcast_to(pieces.astype(BF16), (tq, LANES))
        q_heads.append(jnp.concatenate([jnp.where(keep, q, jnp.zeros_like(q)), extra], axis=1))
    ones_rows = (lax.broadcasted_iota(jnp.int32, (LANES, tq), 0) < BIAS_PIECES).astype(BF16)
    acc_ref[...] = jnp.zeros_like(acc_ref)

    def block(item):
        return i - jnp.minimum(item, i)

    def scores(item, parity):
        for h in range(n_heads):
            if h % 2 == 0:
                kt = jnp.concatenate([k_ref[0, block(item), pair(h), :], ones_rows], axis=0)
            z_ref[parity, h] = _dot(q_heads[h], kt)

    def weights(item, parity, later, mask=None):
        out = []
        for h in range(n_heads):
            z = z_ref[parity, h]
            if mask is not None:
                z = jnp.where(mask, z, MASKED)
            sp = _softplus2(z)
            suf = _dot(sp.astype(BF16), tri_ref[...])
            total = suf + jnp.where(item > i, -MASKED, later[h])
            w = jnp.exp2((z - sp) - total).astype(BF16)
            acc_ref[h] = acc_ref[h] + lax.dot_general(
                w, v_ref[0, block(item), pair(h), :], _NT, preferred_element_type=F32)
            out.append(later[h] + (suf[:, 0:1] + sp[:, 0:1]))
        return tuple(out)

    t_idx = lax.broadcasted_iota(jnp.int32, (tq, tq), 0)
    s_idx = lax.broadcasted_iota(jnp.int32, (tq, tq), 1)
    none_later = jnp.zeros((tq, 1), F32)
    scores(0, 0)
    scores(1, 1)
    later = weights(0, 0, (none_later,) * n_heads, s_idx < t_idx)
    scores(2, 0)

    def trip(m, later):
        n = 2 * m + 1
        later = weights(n, 1, later)
        scores(n + 2, 1)
        later = weights(n + 1, 0, later)
        scores(n + 3, 0)
        return later

    lax.fori_loop(0, (i + 1) // 2, trip, later)
    for h in range(0, n_heads, 2):
        o_ref[0, :, pair(h)] = jnp.where(first_head, acc_ref[h], acc_ref[h + 1]).astype(BF16)


def _attn_prompt(q, ktb, vtb, bias, tri, *, heads_per_step):
    batch, seq, d = q.shape
    tq = ktb.shape[3]
    dh = d // N_HEADS
    width = heads_per_step * dh
    pieces, rest = [], bias
    for _ in range(BIAS_PIECES):
        pieces.append(rest.astype(BF16).astype(F32))
        rest = rest - pieces[-1]
    bias = jnp.concatenate(pieces)
    grid_spec = pltpu.PrefetchScalarGridSpec(
        num_scalar_prefetch=1,
        grid=(batch, d // width, seq // tq),
        in_specs=[
            pl.BlockSpec((1, tq, width), lambda b, hg, i, _: (b, i, hg)),
            pl.BlockSpec((1, seq // tq, width, tq), lambda b, hg, i, _: (b, 0, hg, 0)),
            pl.BlockSpec((1, seq // tq, width, tq), lambda b, hg, i, _: (b, 0, hg, 0)),
            pl.BlockSpec((tq, tq), lambda b, hg, i, _: (0, 0)),
        ],
        out_specs=pl.BlockSpec((1, tq, width), lambda b, hg, i, _: (b, i, hg)),
        scratch_shapes=[
            pltpu.VMEM((2, heads_per_step, tq, tq), F32),
            pltpu.VMEM((heads_per_step, tq, LANES), F32),
        ],
    )
    return pl.pallas_call(
        functools.partial(_attn_prompt_kernel, tq=tq, dh=dh),
        grid_spec=grid_spec,
        out_shape=jax.ShapeDtypeStruct((batch, seq, d), BF16),
        compiler_params=pltpu.CompilerParams(
            dimension_semantics=("arbitrary",) * 3, vmem_limit_bytes=VMEM_LIMIT),
        name="attn_prompt",
    )(bias, q, ktb, vtb, tri)


def _attn_sample_kernel(*refs, n_pages, page, dh):
    pt_ref, q_ref, kn_ref, vn_ref = refs[:4]
    kc_refs = refs[4:4 + n_pages]
    vc_refs = refs[4 + n_pages:4 + 2 * n_pages]
    bias_ref, tri_ref, o_ref = refs[4 + 2 * n_pages:]
    del pt_ref
    d = q_ref.shape[2]
    rows = N_HEADS * SUBLANES

    q_all = jnp.broadcast_to(q_ref[0][None], (N_HEADS, SUBLANES, d)).reshape(rows, d)
    row_head = lax.div(lax.broadcasted_iota(jnp.int32, (rows, d), 0), SUBLANES)
    col_head = lax.div(lax.broadcasted_iota(jnp.int32, (rows, d), 1), dh)
    qbd = jnp.where(row_head == col_head, q_all, 0.0).astype(BF16)

    fill = jnp.zeros((page - SUBLANES, d), F32)
    k_blk = jnp.concatenate([kn_ref[0], fill], axis=0).astype(BF16)
    v_blk = jnp.concatenate([vn_ref[0], fill], axis=0).astype(BF16)
    t_idx = lax.rem(lax.broadcasted_iota(jnp.int32, (rows, page), 0), SUBLANES)
    s_idx = lax.broadcasted_iota(jnp.int32, (rows, page), 1)
    z = lax.dot_general(qbd, k_blk, _NT, preferred_element_type=F32) + bias_ref[...]
    z = jnp.where(s_idx < t_idx, z, MASKED)
    sp = _softplus2(z)
    suf = _dot(sp.astype(BF16), tri_ref[0:page, 0:page])
    acc_new = _dot(jnp.exp2((z - sp) - suf).astype(BF16), v_blk)
    later = suf[:, 0:1] + sp[:, 0:1]

    acc_t = None
    for b in range(n_pages // 2):
        kt = jnp.concatenate([kc_refs[2 * b + 1][0], kc_refs[2 * b][0]], axis=1).astype(BF16)
        vt = jnp.concatenate([vc_refs[2 * b + 1][0], vc_refs[2 * b][0]], axis=1).astype(BF16)
        z = _dot(qbd, kt) + bias_ref[...]
        sp = _softplus2(z)
        suf = _dot(sp.astype(BF16), tri_ref[...])
        w = jnp.exp2((z - sp) - (suf + later)).astype(BF16)
        pv = lax.dot_general(vt, w, _NT, preferred_element_type=F32)
        acc_t = pv if acc_t is None else acc_t + pv
        later = later + (suf[:, 0:1] + sp[:, 0:1])

    acc = acc_new + acc_t.T
    out_head = lax.div(lax.broadcasted_iota(jnp.int32, (SUBLANES, d), 1), dh)
    out = jnp.zeros((SUBLANES, d), F32)
    for h in range(N_HEADS):
        out = jnp.where(out_head == h, acc[h * SUBLANES:(h + 1) * SUBLANES, :], out)
    o_ref[0] = out


def _attn_sample(page_table, q, k_new, v_new, cache_kt, cache_vt, bias_rows, tri):
    n_seq, n_pages = page_table.shape
    _, d, page = cache_kt.shape
    assert n_pages % 2 == 0
    rows = N_HEADS * SUBLANES

    def cache_spec(slot):
        return pl.BlockSpec((1, d, page),
                            lambda s, pt: (pt[s * n_pages + n_pages - 1 - slot], 0, 0))

    seq_spec = pl.BlockSpec((1, SUBLANES, d), lambda s, pt: (s, 0, 0))
    cache_specs = [cache_spec(slot) for slot in range(n_pages)]
    grid_spec = pltpu.PrefetchScalarGridSpec(
        num_scalar_prefetch=1,
        grid=(n_seq,),
        in_specs=[seq_spec, seq_spec, seq_spec] + cache_specs + cache_specs + [
            pl.BlockSpec((rows, 1), lambda s, pt: (0, 0)),
            pl.BlockSpec((2 * page, 2 * page), lambda s, pt: (0, 0)),
        ],
        out_specs=seq_spec,
    )
    return pl.pallas_call(
        functools.partial(_attn_sample_kernel, n_pages=n_pages, page=page, dh=d // N_HEADS),
        grid_spec=grid_spec,
        out_shape=jax.ShapeDtypeStruct((n_seq, SUBLANES, d), F32),
        compiler_params=pltpu.CompilerParams(
            dimension_semantics=("arbitrary",), vmem_limit_bytes=VMEM_LIMIT),
        name="attn_sample",
    )(page_table.reshape(-1), q, k_new, v_new, *([cache_kt] * n_pages), *([cache_vt] * n_pages),
      bias_rows, tri)


def _suffix_ones(n):
    r = lax.broadcasted_iota(jnp.int32, (n, n), 0)
    c = lax.broadcasted_iota(jnp.int32, (n, n), 1)
    return (r > c).astype(BF16)


def kernel(x_prompt, x_sample, cache_k, cache_v, state_conv, state_ffn, page_table, norm_mix, norm_ffn, norm_final, w_sc_in, w_sc_dw, w_sc_out, w_qkv, w_o, sb_bias, w_ffn_up, w_ffn_dw, b_ffn_dw, w_ffn_down):
    bp, tp, d = x_prompt.shape
    bs, ts, _ = x_sample.shape
    dff = w_ffn_down.shape[1]
    dh = d // N_HEADS
    page = cache_k.shape[2]
    hist = CONV_WIDTH - 1
    scale = LOG2_E / math.sqrt(dh)

    w_in, w_out = w_sc_in[0].astype(BF16), w_sc_out[0].astype(BF16)
    wq = w_qkv[0, :, 0:d].astype(BF16)
    wkt = w_qkv[0, :, d:2 * d].T.astype(BF16)
    wvt = w_qkv[0, :, 2 * d:3 * d].T.astype(BF16)
    wo = w_o[0].astype(BF16)
    w_up, w_down = w_ffn_up.astype(BF16), w_ffn_down.astype(BF16)
    g_mix = norm_mix[:, None, :]
    g_ffn = norm_ffn[:, None, :]
    g_fin = norm_final[None, :]
    b_dw = b_ffn_dw[:, None, :]
    bias = sb_bias[0] * LOG2_E

    def trunk(x, conv_prev, ffn_prev, attend, *, stride, tm_mix, tm_ffn, tm_qkv):
        groups, rows, _ = x.shape
        x, conv_new = _mixer0(x, conv_prev, g_mix[0], w_in, w_sc_dw[0], w_out,
                              stride=stride, tm=tm_mix)
        x, ffn_new0 = _ffn(x, ffn_prev[0], g_ffn[0], w_up[0], w_ffn_dw[0], b_dw[0], w_down[0],
                           stride=stride, tm=tm_ffn)
        q, kt, vt, ktb, vtb = _qkv(x, g_mix[1], wq, wkt, wvt, tm=tm_qkv, blk=KEY_BLOCK, scale=scale)
        o = attend(q, kt, vt, ktb, vtb)
        y, ffn_new1 = _ffn(x, ffn_prev[1], g_ffn[1], w_up[1], w_ffn_dw[1], b_dw[1], w_down[1],
                           stride=stride, tm=tm_ffn, oproj=(o, wo), g_final=g_fin)
        return y, conv_new, (ffn_new0, ffn_new1), kt, vt

    def attend_prompt(q, kt, vt, ktb, vtb):
        return _attn_prompt(q, ktb, vtb, bias, _suffix_ones(KEY_BLOCK), heads_per_step=8)

    y_p, conv_p, ffn_p, kt_p, vt_p = trunk(
        x_prompt, jnp.zeros((bp, hist, d), F32), jnp.zeros((2, bp, hist, dff), F32),
        attend_prompt, stride=1, tm_mix=512, tm_ffn=256, tm_qkv=512)

    def to_rows(a):
        return a.transpose(1, 0, 2).reshape(1, a.shape[1] * bs, a.shape[2])

    def from_rows(a, t):
        return a.reshape(t, bs, a.shape[-1]).transpose(1, 0, 2)

    def per_seq_tile(a):
        return jnp.pad(from_rows(a, ts).astype(F32), ((0, 0), (0, SUBLANES - ts), (0, 0)))

    def pages_t(cache):
        return cache.transpose(0, 2, 3, 1).reshape(cache.shape[0], d, page)

    def attend_sample(q, kt, vt, ktb, vtb):
        bias_rows = jnp.repeat(bias, SUBLANES)[:, None]
        o = _attn_sample(page_table, per_seq_tile(q[0]), per_seq_tile(kt[0].T), per_seq_tile(vt[0].T),
                         pages_t(cache_k[0]), pages_t(cache_v[0]),
                         bias_rows, _suffix_ones(2 * page))
        return to_rows(o[:, :ts]).astype(BF16)

    y_s, conv_s, ffn_s, kt_s, vt_s = trunk(
        to_rows(x_sample), to_rows(state_conv[0]), jnp.stack([to_rows(state_ffn[0]), to_rows(state_ffn[1])]),
        attend_sample, stride=bs, tm_mix=ts * bs, tm_ffn=ts * bs, tm_qkv=ts * bs)

    heads = (N_HEADS, dh)

    def prompt_heads(t):
        return t.reshape(bp, *heads, tp).transpose(0, 3, 1, 2)[None]

    return (
        y_p,
        from_rows(y_s[0], ts),
        prompt_heads(kt_p),
        prompt_heads(vt_p),
        conv_p[None],
        jnp.stack(ffn_p),
        from_rows(kt_s[0].T, ts).reshape(1, bs, ts, *heads),
        from_rows(vt_s[0].T, ts).reshape(1, bs, ts, *heads),
        from_rows(conv_s[0], hist)[None],
        jnp.stack([from_rows(f[0], hist) for f in ffn_s]),
    )
```
